```python
import jax, jax.numpy as jnp
from jax import lax
import numpy as np

D_MODEL = 1024
BATCH = 32
SEQ = 2048
DEPTH = 4
DEC_BATCH = 4
DEC_SEQ = 8192
PAST_LEN = 128

N_MIXERS = 2
N_HEADS = 16
N_KV_HEADS = 4
HEAD_DIM = D_MODEL // N_HEADS
KV_GROUP = N_HEADS // N_KV_HEADS
QKV_DIM = (N_HEADS + 2 * N_KV_HEADS) * HEAD_DIM
ROPE_AXIS_DIM = HEAD_DIM // 2
ROPE_THETA = 10000.0
Q_BLOCK = 128
FNET_GROUPS = 8
FNET_GROUP_DIM = D_MODEL // FNET_GROUPS
D_FF = 2816
GRID_W = 64
N_SUBLAYERS = 3
N_MOD = 3
N_ATTN_LAYERS = (DEPTH + 1) // 2
N_FNET_LAYERS = DEPTH // 2
EPS = 1e-6

kernel_name = "hybrid_gqa_fnet_macaron_adaln_encoder"


def rms_norm(x, gain):
    xf = x.astype(jnp.float32)
    y = xf * lax.rsqrt(jnp.mean(xf * xf, axis=-1, keepdims=True) + EPS)
    return (y * gain.astype(jnp.float32)).astype(x.dtype)


def axial_rope_tables(n_tokens):
    rows = n_tokens // GRID_W
    r = jnp.repeat(jnp.arange(rows), GRID_W).astype(jnp.float32)
    c = jnp.tile(jnp.arange(GRID_W), rows).astype(jnp.float32)
    inv = ROPE_THETA ** (-jnp.arange(0, ROPE_AXIS_DIM, 2, dtype=jnp.float32) / ROPE_AXIS_DIM)
    ang = jnp.concatenate([r[:, None] * inv, c[:, None] * inv], axis=-1)
    return jnp.cos(ang), jnp.sin(ang)


def apply_rope(x, cos, sin):
    xf = x.astype(jnp.float32).reshape(*x.shape[:-1], HEAD_DIM // 2, 2)
    x0, x1 = xf[..., 0], xf[..., 1]
    out = jnp.stack([x0 * cos - x1 * sin, x0 * sin + x1 * cos], axis=-1)
    return out.reshape(x.shape).astype(x.dtype)


def gqa_attention(h, w_qkv, q_gain, k_gain, w_o):
    b, s, _ = h.shape
    qkv = h @ w_qkv
    q, k, v = jnp.split(qkv, [N_HEADS * HEAD_DIM, (N_HEADS + N_KV_HEADS) * HEAD_DIM], axis=-1)
    q = q.reshape(b, s, N_KV_HEADS, KV_GROUP, HEAD_DIM)
    k = k.reshape(b, s, N_KV_HEADS, HEAD_DIM)
    v = v.reshape(b, s, N_KV_HEADS, HEAD_DIM)
    cos, sin = axial_rope_tables(s)
    q = apply_rope(rms_norm(q, q_gain), cos[:, None, None, :], sin[:, None, None, :])
    k = apply_rope(rms_norm(k, k_gain), cos[:, None, :], sin[:, None, :])
    scale = 1.0 / np.sqrt(HEAD_DIM)
    n_blk = s // Q_BLOCK
    qb = jnp.moveaxis(q.reshape(b, n_blk, Q_BLOCK, N_KV_HEADS, KV_GROUP, HEAD_DIM), 1, 0)

    def attend(q_blk):
        sc = jnp.einsum('bqkgd,bskd->bkgqs', q_blk, k, preferred_element_type=jnp.float32) * scale
        p = jax.nn.softmax(sc, axis=-1)
        return jnp.einsum('bkgqs,bskd->bqkgd', p.astype(v.dtype), v)

    o = lax.map(attend, qb)
    o = jnp.moveaxis(o, 0, 1).reshape(b, s, N_HEADS * HEAD_DIM)
    return o @ w_o


def fourier_mix(h, w_o):
    b, s, _ = h.shape
    hg = h.astype(jnp.float32).reshape(b, s, FNET_GROUPS, FNET_GROUP_DIM)
    f = jnp.fft.fft2(hg, axes=(1, 3), norm="ortho").real
    return f.reshape(b, s, D_MODEL).astype(h.dtype) @ w_o


def swiglu(h, w_in, w_out):
    g, u = jnp.split(h @ w_in, 2, axis=-1)
    return (jax.nn.silu(g) * u) @ w_out


def run_trunk(x, c, norm_gain, ada_w, ada_b, ffn_w_in, ffn_w_out,
              attn_w_qkv, attn_q_gain, attn_k_gain, attn_w_o, fnet_w_o):
    b = x.shape[0]
    for l in range(DEPTH):
        mod = (jax.nn.silu(c) @ ada_w[l] + ada_b[l]).reshape(b, N_SUBLAYERS, N_MOD, D_MODEL)
        mod = mod[:, :, :, None, :]

        def modnorm(y, j):
            return rms_norm(y, norm_gain[l, j]) * (1.0 + mod[:, j, 1]) + mod[:, j, 0]

        x = x + 0.5 * mod[:, 0, 2] * swiglu(modnorm(x, 0), ffn_w_in[l, 0], ffn_w_out[l, 0])
        h = modnorm(x, 1)
        if l % N_MIXERS == 0:
            a = l // N_MIXERS
            mixed = gqa_attention(h, attn_w_qkv[a], attn_q_gain[a], attn_k_gain[a], attn_w_o[a])
        else:
            mixed = fourier_mix(h, fnet_w_o[l // N_MIXERS])
        x = x + mod[:, 1, 2] * mixed
        x = x + 0.5 * mod[:, 2, 2] * swiglu(modnorm(x, 2), ffn_w_in[l, 1], ffn_w_out[l, 1])
    return x


def setup_inputs(seed: int = 0) -> dict:
    key = jax.random.key(seed)
    ks = jax.random.split(key, 16)
    f32 = jnp.float32
    nrm = lambda k, shape, s: jax.random.normal(k, shape, f32) * s
    return {
        "x_prompt": nrm(ks[0], (BATCH, SEQ, D_MODEL), 1.0),
        "x_sample": nrm(ks[1], (DEC_BATCH, DEC_SEQ, D_MODEL), 1.0),
        "c_prompt": nrm(ks[2], (BATCH, D_MODEL), 1.0),
        "c_sample": nrm(ks[3], (DEC_BATCH, D_MODEL), 1.0),
        "norm_gain": 1.0 + nrm(ks[4], (DEPTH, N_SUBLAYERS, D_MODEL), 0.01),
        "ada_w": nrm(ks[5], (DEPTH, D_MODEL, N_SUBLAYERS * N_MOD * D_MODEL), 0.5 * D_MODEL ** -0.5),
        "ada_b": nrm(ks[6], (DEPTH, N_SUBLAYERS * N_MOD * D_MODEL), 0.01),
        "ffn_w_in": nrm(ks[7], (DEPTH, 2, D_MODEL, 2 * D_FF), D_MODEL ** -0.5),
        "ffn_w_out": nrm(ks[8], (DEPTH, 2, D_FF, D_MODEL), D_FF ** -0.5),
        "attn_w_qkv": nrm(ks[9], (N_ATTN_LAYERS, D_MODEL, QKV_DIM), D_MODEL ** -0.5),
        "attn_q_gain": 1.0 + nrm(ks[10], (N_ATTN_LAYERS, HEAD_DIM), 0.01),
        "attn_k_gain": 1.0 + nrm(ks[11], (N_ATTN_LAYERS, HEAD_DIM), 0.01),
        "attn_w_o": nrm(ks[12], (N_ATTN_LAYERS, N_HEADS * HEAD_DIM, D_MODEL), D_MODEL ** -0.5),
        "fnet_w_o": nrm(ks[13], (N_FNET_LAYERS, D_MODEL, D_MODEL), D_MODEL ** -0.5),
    }


def reference(x_prompt, x_sample, c_prompt, c_sample, norm_gain, ada_w, ada_b, ffn_w_in, ffn_w_out,
              attn_w_qkv, attn_q_gain, attn_k_gain, attn_w_o, fnet_w_o):
    y_prompt = run_trunk(x_prompt, c_prompt, norm_gain, ada_w, ada_b, ffn_w_in, ffn_w_out,
                         attn_w_qkv, attn_q_gain, attn_k_gain, attn_w_o, fnet_w_o)
    y_sample = run_trunk(x_sample, c_sample, norm_gain, ada_w, ada_b, ffn_w_in, ffn_w_out,
                         attn_w_qkv, attn_q_gain, attn_k_gain, attn_w_o, fnet_w_o)
    return (y_prompt, y_sample)
```

```python
import functools
import math

import numpy as np
import jax
import jax.numpy as jnp
from jax import lax
from jax.experimental import pallas as pl
from jax.experimental.pallas import tpu as pltpu

F32 = jnp.float32
BF16 = jnp.bfloat16

D_MODEL = 1024
DEPTH = 4
N_HEADS = 16
N_KV_HEADS = 4
KV_GROUP = N_HEADS // N_KV_HEADS
HEAD_DIM = D_MODEL // N_HEADS
Q_DIM = N_HEADS * HEAD_DIM
KV_DIM = N_KV_HEADS * HEAD_DIM
QKV_DIM = Q_DIM + 2 * KV_DIM
GROUP_W = KV_GROUP * HEAD_DIM
ROPE_AXIS_DIM = HEAD_DIM // 2
ROPE_THETA = 10000.0
FNET_GROUPS = 8
FNET_GROUP_DIM = D_MODEL // FNET_GROUPS
D_FF = 2816
GRID_W = 64
N_SUBLAYERS = 3
N_MOD = 3
EPS = 1e-6
LOG2E = 1.4426950408889634

LANES = 128
VMEM_LIMIT = 48 * 1024 * 1024

FFN_TM = 512
FFN_TF = D_FF // 2
TOK_TM = 512
ATTN_TQ = 128
ATTN_TK = 2048
DFT_N2 = 64
DFT_KB = 8
DFT1_TC = 4096


def _params(*sem):
    return pltpu.CompilerParams(dimension_semantics=sem, vmem_limit_bytes=VMEM_LIMIT)


def _dot(a, b):
    return jnp.dot(a, b, preferred_element_type=F32)


def _modnorm(x, gain, shift, scale):
    ms = jnp.mean(x * x, axis=-1, keepdims=True)
    return (x * lax.rsqrt(ms + EPS) * gain) * (1.0 + scale) + shift


def _mod_kernel(c_ref, w_ref, b_ref, o_ref):
    c = c_ref[...]
    a = (c * jax.nn.sigmoid(c)).astype(BF16)
    o_ref[...] = _dot(a, w_ref[...].astype(BF16)) + b_ref[...]


def _modulation(c, ada_w, ada_b):
    b = c.shape[0]
    n = N_SUBLAYERS * N_MOD
    out = pl.pallas_call(
        _mod_kernel,
        grid=(DEPTH, n),
        in_specs=[
            pl.BlockSpec((b, D_MODEL), lambda l, j: (0, 0)),
            pl.BlockSpec((None, D_MODEL, D_MODEL), lambda l, j: (l, 0, j)),
            pl.BlockSpec((None, 1, D_MODEL), lambda l, j: (l, 0, j)),
        ],
        out_specs=pl.BlockSpec((None, b, D_MODEL), lambda l, j: (l, 0, j)),
        out_shape=jax.ShapeDtypeStruct((DEPTH, b, n * D_MODEL), F32),
        compiler_params=_params("parallel", "parallel"),
        name="adaln_mod",
    )(c, ada_w, ada_b.reshape(DEPTH, 1, n * D_MODEL))
    return out.reshape(DEPTH, b, n, D_MODEL)


def _ffn_kernel(x_ref, mod_ref, gain_ref, wg_ref, wu_ref, wo_ref, o_ref, h_ref, *, sub):
    c = pl.program_id(2)
    mod = mod_ref[...]
    gate = mod[3 * sub + 2:3 * sub + 3]

    @pl.when(c == 0)
    def _():
        h = _modnorm(x_ref[...], gain_ref[...], mod[3 * sub:3 * sub + 1], mod[3 * sub + 1:3 * sub + 2])
        h_ref[...] = h.astype(BF16)

    h = h_ref[...]
    g = _dot(h, wg_ref[...])
    u = _dot(h, wu_ref[...])
    a = (g * jax.nn.sigmoid(g) * u).astype(BF16)
    part = (0.5 * gate) * _dot(a, wo_ref[...])

    @pl.when(c == 0)
    def _():
        o_ref[...] = x_ref[...] + part

    @pl.when(c != 0)
    def _():
        o_ref[...] += part


def _ffn(x, mod, gain, w_in, w_out, layer, which, sub):
    b, s, _ = x.shape
    tm = min(FFN_TM, s)
    nf = D_FF // FFN_TF
    return pl.pallas_call(
        functools.partial(_ffn_kernel, sub=sub),
        grid=(b, s // tm, nf),
        in_specs=[
            pl.BlockSpec((None, tm, D_MODEL), lambda i, t, c: (i, t, 0)),
            pl.BlockSpec((None, None, N_SUBLAYERS * N_MOD, D_MODEL), lambda i, t, c: (layer, i, 0, 0)),
            pl.BlockSpec((1, D_MODEL), lambda i, t, c: (0, 0)),
            pl.BlockSpec((None, None, D_MODEL, FFN_TF), lambda i, t, c: (layer, which, 0, c)),
            pl.BlockSpec((None, None, D_MODEL, FFN_TF), lambda i, t, c: (layer, which, 0, nf + c)),
            pl.BlockSpec((None, None, FFN_TF, D_MODEL), lambda i, t, c: (layer, which, c, 0)),
        ],
        out_specs=pl.BlockSpec((None, tm, D_MODEL), lambda i, t, c: (i, t, 0)),
        out_shape=jax.ShapeDtypeStruct(x.shape, F32),
        scratch_shapes=[pltpu.VMEM((tm, D_MODEL), BF16)],
        compiler_params=_params("parallel", "parallel", "arbitrary"),
        name=f"ffn_l{layer}_{which}",
    )(x, mod, gain, w_in, w_in, w_out)


def _head_rms(t, ones_bd, gain):
    t2 = t * t
    hi = t2.astype(BF16)
    lo = (t2 - hi.astype(F32)).astype(BF16)
    ss = _dot(hi, ones_bd) + _dot(lo, ones_bd)
    return t * lax.rsqrt(ss * (1.0 / HEAD_DIM) + EPS) * gain


def _rope(t, cos, sin_signed):
    n = t.shape[1]
    lane = lax.broadcasted_iota(jnp.int32, t.shape, 1)
    nxt = pltpu.roll(t, n - 1, 1)
    prv = pltpu.roll(t, 1, 1)
    partner = jnp.where(lane % 2 == 0, nxt, prv)
    reps = n // LANES
    cosf = jnp.concatenate([cos] * reps, axis=1)
    sinf = jnp.concatenate([sin_signed] * reps, axis=1)
    return t * cosf + partner * sinf


def _repeat_heads(t):
    lane = lax.broadcasted_iota(jnp.int32, (t.shape[0], LANES), 1)
    low = lane < HEAD_DIM
    cols = []
    for j in range(KV_DIM // LANES):
        c = t[:, j * LANES:(j + 1) * LANES]
        sw = pltpu.roll(c, HEAD_DIM, 1)
        first = jnp.where(low, c, sw)
        second = jnp.where(low, sw, c)
        cols += [first] * (GROUP_W // LANES) + [second] * (GROUP_W // LANES)
    return jnp.concatenate(cols, axis=1)


def _qkv_kernel(x_ref, mod_ref, gain_ref, w_ref, qg_ref, kg_ref, cos_ref, sin_ref, bd_ref,
                q_ref, k_ref, v_ref):
    mod = mod_ref[...]
    h = _modnorm(x_ref[...], gain_ref[...], mod[3:4], mod[4:5]).astype(BF16)
    qkv = _dot(h, w_ref[...])
    q = qkv[:, :Q_DIM]
    k = qkv[:, Q_DIM:Q_DIM + KV_DIM]
    v = qkv[:, Q_DIM + KV_DIM:]
    cos = cos_ref[...]
    sin = sin_ref[...]
    q = _rope(_head_rms(q, bd_ref[...], qg_ref[...]), cos, sin)
    k = _rope(_head_rms(k, bd_ref[:KV_DIM, :KV_DIM], kg_ref[...]), cos, sin)
    q_ref[...] = (q * (LOG2E / math.sqrt(HEAD_DIM))).astype(BF16)
    k_ref[...] = _repeat_heads(k).astype(BF16)
    v_ref[...] = _repeat_heads(v).astype(BF16)


def _rope_tables(s):
    rows = s // GRID_W
    r = jnp.repeat(jnp.arange(rows), GRID_W).astype(F32)
    c = jnp.tile(jnp.arange(GRID_W), rows).astype(F32)
    inv = ROPE_THETA ** (-jnp.arange(0, ROPE_AXIS_DIM, 2, dtype=F32) / ROPE_AXIS_DIM)
    ang = jnp.concatenate([r[:, None] * inv, c[:, None] * inv], axis=-1)
    cos = jnp.repeat(jnp.cos(ang), 2, axis=-1)
    sin = jnp.repeat(jnp.sin(ang), 2, axis=-1)
    sign = jnp.tile(jnp.array([-1.0, 1.0], F32), HEAD_DIM // 2)
    reps = LANES // HEAD_DIM
    return jnp.tile(cos, (1, reps)), jnp.tile(sin * sign, (1, reps))


def _qkv(x, mod, gain, w_qkv, q_gain, k_gain, cos, sin, ones_bd, layer, a):
    b, s, _ = x.shape
    tm = min(TOK_TM, s)
    tok = lambda i, t: (i, t, 0)
    const2 = lambda i, t: (0, 0)
    shape = jax.ShapeDtypeStruct((b, s, D_MODEL), BF16)
    return pl.pallas_call(
        _qkv_kernel,
        grid=(b, s // tm),
        in_specs=[
            pl.BlockSpec((None, tm, D_MODEL), tok),
            pl.BlockSpec((None, None, N_SUBLAYERS * N_MOD, D_MODEL), lambda i, t: (layer, i, 0, 0)),
            pl.BlockSpec((1, D_MODEL), const2),
            pl.BlockSpec((None, D_MODEL, QKV_DIM), lambda i, t: (a, 0, 0)),
            pl.BlockSpec((1, Q_DIM), const2),
            pl.BlockSpec((1, KV_DIM), const2),
            pl.BlockSpec((tm, LANES), lambda i, t: (t, 0)),
            pl.BlockSpec((tm, LANES), lambda i, t: (t, 0)),
            pl.BlockSpec((Q_DIM, Q_DIM), const2),
        ],
        out_specs=[pl.BlockSpec((None, tm, D_MODEL), tok)] * 3,
        out_shape=[shape] * 3,
        compiler_params=_params("parallel", "parallel"),
        name=f"qkv_l{layer}",
    )(x, mod, gain, w_qkv, q_gain, k_gain, cos, sin, ones_bd)


def _attn_kernel(q_ref, k_ref, v_ref, o_ref, *, tq, tk, nk):
    q = q_ref[...]
    head = lax.broadcasted_iota(jnp.int32, q.shape, 1) // HEAD_DIM
    zero = jnp.zeros_like(q)
    qs = jnp.concatenate([jnp.where(head == g, q, zero) for g in range(KV_GROUP)], axis=0)

    def chunk(i, carry):
        m, l, acc = carry
        kc = k_ref[pl.ds(i * tk, tk), :]
        vc = v_ref[pl.ds(i * tk, tk), :]
        s = lax.dot_general(qs, kc, (((1,), (1,)), ((), ())), preferred_element_type=F32)
        m_new = jnp.maximum(m, jnp.max(s, axis=1, keepdims=True))
        alpha = jnp.exp2(m - m_new)
        p = jnp.exp2(s - m_new)
        l = alpha * l + jnp.sum(p, axis=1, keepdims=True)
        acc = alpha * acc + _dot(p.astype(BF16), vc)
        return m_new, l, acc

    rows = KV_GROUP * tq
    init = (jnp.full((rows, 1), -1e30, F32), jnp.zeros((rows, 1), F32), jnp.zeros((rows, GROUP_W), F32))
    if nk == 1:
        m, l, acc = chunk(0, init)
    else:
        m, l, acc = lax.fori_loop(0, nk, chunk, init)
    out = acc / l
    o = jnp.zeros((tq, GROUP_W), F32)
    for g in range(KV_GROUP):
        o = o + jnp.where(head == g, out[g * tq:(g + 1) * tq], 0.0)
    o_ref[...] = o.astype(BF16)


def _attention(q, k4, v4, layer):
    b, s, _ = q.shape
    tq = min(ATTN_TQ, s)
    tk = min(ATTN_TK, s)
    return pl.pallas_call(
        functools.partial(_attn_kernel, tq=tq, tk=tk, nk=s // tk),
        grid=(b, N_KV_HEADS, s // tq),
        in_specs=[
            pl.BlockSpec((None, tq, GROUP_W), lambda i, j, t: (i, t, j)),
            pl.BlockSpec((None, s, GROUP_W), lambda i, j, t: (i, 0, j)),
            pl.BlockSpec((None, s, GROUP_W), lambda i, j, t: (i, 0, j)),
        ],
        out_specs=pl.BlockSpec((None, tq, GROUP_W), lambda i, j, t: (i, t, j)),
        out_shape=jax.ShapeDtypeStruct((b, s, D_MODEL), BF16),
        compiler_params=_params("parallel", "parallel", "parallel"),
        name=f"attn_l{layer}",
    )(q, k4, v4)


def _proj_kernel(o_ref, x_ref, mod_ref, w_ref, out_ref):
    gate = mod_ref[...][5:6]
    out_ref[...] = x_ref[...] + gate * _dot(o_ref[...], w_ref[...])


def _proj_residual(o, x, mod, w_o, layer, a):
    b, s, _ = x.shape
    tm = min(TOK_TM, s)
    tok = lambda i, t: (i, t, 0)
    return pl.pallas_call(
        _proj_kernel,
        grid=(b, s // tm),
        in_specs=[
            pl.BlockSpec((None, tm, D_MODEL), tok),
            pl.BlockSpec((None, tm, D_MODEL), tok),
            pl.BlockSpec((None, None, N_SUBLAYERS * N_MOD, D_MODEL), lambda i, t: (layer, i, 0, 0)),
            pl.BlockSpec((None, D_MODEL, D_MODEL), lambda i, t: (a, 0, 0)),
        ],
        out_specs=pl.BlockSpec((None, tm, D_MODEL), tok),
        out_shape=jax.ShapeDtypeStruct(x.shape, F32),
        compiler_params=_params("parallel", "parallel"),
        name=f"attn_out_l{layer}",
    )(o, x, mod, w_o)


def _dft_tables(s):
    n2 = DFT_N2
    n1 = s // n2
    k1 = np.arange(n1)
    ang1 = 2.0 * np.pi * ((k1[:, None] * k1[None, :]) % n1) / n1
    m1 = np.concatenate([np.cos(ang1), -np.sin(ang1)], axis=0)
    s2 = np.arange(n2)
    num = (s2[None, None, :] * k1[:, None, None] + s2[None, None, :] * s2[None, :, None] * n1) % s
    psi = 2.0 * np.pi * num / s
    tr, ti = np.cos(psi), -np.sin(psi)
    t = np.concatenate([np.concatenate([tr, -ti], axis=2), np.concatenate([ti, tr], axis=2)], axis=1)
    c = np.arange(FNET_GROUP_DIM)
    phi = 2.0 * np.pi * ((c[:, None] * c[None, :]) % FNET_GROUP_DIM) / FNET_GROUP_DIM
    cs = np.concatenate([np.cos(phi), np.sin(phi)], axis=0)
    return jnp.asarray(m1, BF16), jnp.asarray(t, BF16), jnp.asarray(cs, BF16)


def _modnorm_kernel(x_ref, mod_ref, gain_ref, h_ref):
    mod = mod_ref[...]
    h_ref[...] = _modnorm(x_ref[...], gain_ref[...], mod[3:4], mod[4:5]).astype(BF16)


def _mixer_norm(x, mod, gain, layer):
    b, s, _ = x.shape
    tm = min(TOK_TM, s)
    tok = lambda i, t: (i, t, 0)
    return pl.pallas_call(
        _modnorm_kernel,
        grid=(b, s // tm),
        in_specs=[
            pl.BlockSpec((None, tm, D_MODEL), tok),
            pl.BlockSpec((None, None, N_SUBLAYERS * N_MOD, D_MODEL), lambda i, t: (layer, i, 0, 0)),
            pl.BlockSpec((1, D_MODEL), lambda i, t: (0, 0)),
        ],
        out_specs=pl.BlockSpec((None, tm, D_MODEL), tok),
        out_shape=jax.ShapeDtypeStruct(x.shape, BF16),
        compiler_params=_params("parallel", "parallel"),
        name=f"fnet_norm_l{layer}",
    )(x, mod, gain)


def _dft1_kernel(m_ref, h_ref, y_ref):
    y_ref[...] = _dot(m_ref[...], h_ref[...]).astype(BF16)


def _dft1(h, m1, layer):
    b, n1, cols = h.shape
    tc = min(DFT1_TC, cols)
    return pl.pallas_call(
        _dft1_kernel,
        grid=(b, cols // tc),
        in_specs=[
            pl.BlockSpec((2 * n1, n1), lambda i, t: (0, 0)),
            pl.BlockSpec((None, n1, tc), lambda i, t: (i, 0, t)),
        ],
        out_specs=pl.BlockSpec((None, 2 * n1, tc), lambda i, t: (i, 0, t)),
        out_shape=jax.ShapeDtypeStruct((b, 2 * n1, cols), BF16),
        compiler_params=_params("parallel", "parallel"),
        name=f"fnet_dft1_l{layer}",
    )(m1, h)


def _dft2_kernel(t_ref, y_ref, u_ref, *, kb, n2):
    for i in range(kb):
        y = jnp.concatenate([y_ref[0, i], y_ref[1, i]], axis=0)
        u = _dot(t_ref[i], y)
        u_ref[0, :, i * D_MODEL:(i + 1) * D_MODEL] = u[:n2].astype(BF16)
        u_ref[1, :, i * D_MODEL:(i + 1) * D_MODEL] = u[n2:].astype(BF16)


def _dft2(y, t, layer):
    b, _, n1, n2, _ = y.shape
    kb = min(DFT_KB, n1)
    return pl.pallas_call(
        functools.partial(_dft2_kernel, kb=kb, n2=n2),
        grid=(b, n1 // kb),
        in_specs=[
            pl.BlockSpec((kb, 2 * n2, 2 * n2), lambda i, t: (t, 0, 0)),
            pl.BlockSpec((None, 2, kb, n2, D_MODEL), lambda i, t: (i, 0, t, 0, 0)),
        ],
        out_specs=pl.BlockSpec((None, 2, n2, kb * D_MODEL), lambda i, t: (i, 0, 0, t)),
        out_shape=jax.ShapeDtypeStruct((b, 2, n2, n1 * D_MODEL), BF16),
        compiler_params=_params("parallel", "parallel"),
        name=f"fnet_dft2_l{layer}",
    )(t, y)


def _dft3_kernel(u_ref, x_ref, mod_ref, cs_ref, w_ref, out_ref, *, inv_norm):
    ur = u_ref[0]
    ui = u_ref[1]
    cs = cs_ref[...]
    parts = []
    for g in range(FNET_GROUPS):
        cols = slice(g * FNET_GROUP_DIM, (g + 1) * FNET_GROUP_DIM)
        parts.append(_dot(jnp.concatenate([ur[:, cols], ui[:, cols]], axis=1), cs))
    f = (jnp.concatenate(parts, axis=1) * inv_norm).astype(BF16)
    gate = mod_ref[...][5:6]
    out_ref[...] = x_ref[...] + gate * _dot(f, w_ref[...])


def _dft3_residual(u, x, mod, cs, w_o, layer, a):
    b, s, _ = x.shape
    tm = min(TOK_TM, s)
    tok = lambda i, t: (i, t, 0)
    inv_norm = 1.0 / math.sqrt(s * FNET_GROUP_DIM)
    return pl.pallas_call(
        functools.partial(_dft3_kernel, inv_norm=inv_norm),
        grid=(b, s // tm),
        in_specs=[
            pl.BlockSpec((None, 2, tm, D_MODEL), lambda i, t: (i, 0, t, 0)),
            pl.BlockSpec((None, tm, D_MODEL), tok),
            pl.BlockSpec((None, None, N_SUBLAYERS * N_MOD, D_MODEL), lambda i, t: (layer, i, 0, 0)),
            pl.BlockSpec((2 * FNET_GROUP_DIM, FNET_GROUP_DIM), lambda i, t: (0, 0)),
            pl.BlockSpec((None, D_MODEL, D_MODEL), lambda i, t: (a, 0, 0)),
        ],
        out_specs=pl.BlockSpec((None, tm, D_MODEL), tok),
        out_shape=jax.ShapeDtypeStruct(x.shape, F32),
        compiler_params=_params("parallel", "parallel"),
        name=f"fnet_out_l{layer}",
    )(u, x, mod, cs, w_o)


def _fourier_mix(x, mod, gain, w_o, tables, layer, a):
    b, s, _ = x.shape
    m1, t, cs = tables
    n2 = DFT_N2
    n1 = s // n2
    h = _mixer_norm(x, mod, gain, layer)
    y = _dft1(h.reshape(b, n1, n2 * D_MODEL), m1, layer)
    u = _dft2(y.reshape(b, 2, n1, n2, D_MODEL), t, layer)
    return _dft3_residual(u.reshape(b, 2, s, D_MODEL), x, mod, cs, w_o, layer, a)


def _run_trunk(x, mod, w):
    s = x.shape[1]
    cos, sin = _rope_tables(s)
    tables = _dft_tables(s)
    for l in range(DEPTH):
        gains = w["norm_gain"][l]
        x = _ffn(x, mod, gains[0:1], w["ffn_w_in"], w["ffn_w_out"], l, 0, 0)
        a = l // 2
        if l % 2 == 0:
            q, k4, v4 = _qkv(x, mod, gains[1:2], w["attn_w_qkv"], w["q_gain"][a], w["k_gain"][a],
                             cos, sin, w["ones_bd"], l, a)
            o = _attention(q, k4, v4, l)
            x = _proj_residual(o, x, mod, w["attn_w_o"], l, a)
        else:
            x = _fourier_mix(x, mod, gains[1:2], w["fnet_w_o"], tables, l, a)
        x = _ffn(x, mod, gains[2:3], w["ffn_w_in"], w["ffn_w_out"], l, 1, 2)
    return x


def kernel(x_prompt, x_sample, c_prompt, c_sample, norm_gain, ada_w, ada_b, ffn_w_in, ffn_w_out,
           attn_w_qkv, attn_q_gain, attn_k_gain, attn_w_o, fnet_w_o):
    head = np.arange(Q_DIM) // HEAD_DIM
    w = {
        "norm_gain": norm_gain,
        "ffn_w_in": ffn_w_in.astype(BF16),
        "ffn_w_out": ffn_w_out.astype(BF16),
        "attn_w_qkv": attn_w_qkv.astype(BF16),
        "attn_w_o": attn_w_o.astype(BF16),
        "fnet_w_o": fnet_w_o.astype(BF16),
        "q_gain": jnp.tile(attn_q_gain, (1, N_HEADS))[:, None, :],
        "k_gain": jnp.tile(attn_k_gain, (1, N_KV_HEADS))[:, None, :],
        "ones_bd": jnp.asarray(head[:, None] == head[None, :], BF16),
    }
    nb = x_prompt.shape[0]
    mod = _modulation(jnp.concatenate([c_prompt, c_sample], axis=0), ada_w, ada_b)
    y_prompt = _run_trunk(x_prompt, mod[:, :nb], w)
    y_sample = _run_trunk(x_sample, mod[:, nb:], w)
    return (y_prompt, y_sample)
```

```python
import functools
import math

import numpy as np
import jax
import jax.numpy as jnp
from jax import lax
from jax.experimental import pallas as pl
from jax.experimental.pallas import tpu as pltpu

F32 = jnp.float32
BF16 = jnp.bfloat16

D_MODEL = 1024
DEPTH = 4
N_HEADS = 16
N_KV_HEADS = 4
KV_GROUP = N_HEADS // N_KV_HEADS
HEAD_DIM = D_MODEL // N_HEADS
Q_DIM = N_HEADS * HEAD_DIM
KV_DIM = N_KV_HEADS * HEAD_DIM
QKV_DIM = Q_DIM + 2 * KV_DIM
GROUP_W = KV_GROUP * HEAD_DIM
ROPE_AXIS_DIM = HEAD_DIM // 2
ROPE_THETA = 10000.0
FNET_GROUPS = 8
FNET_GROUP_DIM = D_MODEL // FNET_GROUPS
D_FF = 2816
GRID_W = 64
N_SUBLAYERS = 3
N_MOD = 3
EPS = 1e-6
LOG2E = 1.4426950408889634

LANES = 128
VMEM_LIMIT = 48 * 1024 * 1024

FFN_TM = 512
TOK_TM = 512
ATTN_TQ = 256
ATTN_TK = 2048
VT_ROWS = HEAD_DIM + 16
DFT_N2 = 64
DFT_KB = 8
DFT1_TC = 4096


def _params(*sem):
    return pltpu.CompilerParams(dimension_semantics=sem, vmem_limit_bytes=VMEM_LIMIT)


def _dot(a, b):
    return jnp.dot(a, b, preferred_element_type=F32)


def _modnorm(x, gain, shift, scale):
    ms = jnp.mean(x * x, axis=-1, keepdims=True)
    return (x * lax.rsqrt(ms + EPS) * gain) * (1.0 + scale) + shift


def _mod_kernel(c_ref, w_ref, b_ref, o_ref):
    c = c_ref[...]
    a = (c * jax.nn.sigmoid(c)).astype(BF16)
    o_ref[...] = _dot(a, w_ref[...].astype(BF16)) + b_ref[...]


def _modulation(c, ada_w, ada_b):
    b = c.shape[0]
    n = N_SUBLAYERS * N_MOD
    out = pl.pallas_call(
        _mod_kernel,
        grid=(DEPTH, n),
        in_specs=[
            pl.BlockSpec((b, D_MODEL), lambda l, j: (0, 0)),
            pl.BlockSpec((None, D_MODEL, D_MODEL), lambda l, j: (l, 0, j)),
            pl.BlockSpec((None, 1, D_MODEL), lambda l, j: (l, 0, j)),
        ],
        out_specs=pl.BlockSpec((None, b, D_MODEL), lambda l, j: (l, 0, j)),
        out_shape=jax.ShapeDtypeStruct((DEPTH, b, n * D_MODEL), F32),
        compiler_params=_params("parallel", "parallel"),
        name="adaln_mod",
    )(c, ada_w, ada_b.reshape(DEPTH, 1, n * D_MODEL))
    return out.reshape(DEPTH, b, n, D_MODEL)


def _ffn_kernel(x_ref, mod_ref, gain_ref, wg_ref, wu_ref, wo_ref, o_ref, *, sub):
    x = x_ref[...]
    mod = mod_ref[...]
    h = _modnorm(x, gain_ref[...], mod[3 * sub:3 * sub + 1], mod[3 * sub + 1:3 * sub + 2]).astype(BF16)
    g = _dot(h, wg_ref[...])
    u = _dot(h, wu_ref[...])
    a = (g * jax.nn.sigmoid(g) * u).astype(BF16)
    o_ref[...] = x + (0.5 * mod[3 * sub + 2:3 * sub + 3]) * _dot(a, wo_ref[...])


def _resident(block_shape, index_map):
    return pl.BlockSpec(block_shape, index_map, pipeline_mode=pl.Buffered(1))


def _ffn(x, mod, gain, w_in, w_out, layer, which, sub):
    b, s, _ = x.shape
    tm = min(FFN_TM, s)
    return pl.pallas_call(
        functools.partial(_ffn_kernel, sub=sub),
        grid=(b, s // tm),
        in_specs=[
            pl.BlockSpec((None, tm, D_MODEL), lambda i, t: (i, t, 0)),
            pl.BlockSpec((None, None, N_SUBLAYERS * N_MOD, D_MODEL), lambda i, t: (layer, i, 0, 0)),
            pl.BlockSpec((1, D_MODEL), lambda i, t: (0, 0)),
            _resident((None, None, D_MODEL, D_FF), lambda i, t: (layer, which, 0, 0)),
            _resident((None, None, D_MODEL, D_FF), lambda i, t: (layer, which, 0, 1)),
            _resident((None, None, D_FF, D_MODEL), lambda i, t: (layer, which, 0, 0)),
        ],
        out_specs=pl.BlockSpec((None, tm, D_MODEL), lambda i, t: (i, t, 0)),
        out_shape=jax.ShapeDtypeStruct(x.shape, F32),
        compiler_params=_params("parallel", "parallel"),
        name=f"ffn_l{layer}_{which}",
    )(x, mod, gain, w_in, w_in, w_out)


def _head_rms(t, ones_bd, gain):
    t2 = t * t
    hi = t2.astype(BF16)
    lo = (t2 - hi.astype(F32)).astype(BF16)
    ss = _dot(hi, ones_bd) + _dot(lo, ones_bd)
    return t * lax.rsqrt(ss * (1.0 / HEAD_DIM) + EPS) * gain


def _rope(t, cos, sin_signed):
    n = t.shape[1]
    lane = lax.broadcasted_iota(jnp.int32, t.shape, 1)
    nxt = pltpu.roll(t, n - 1, 1)
    prv = pltpu.roll(t, 1, 1)
    partner = jnp.where(lane % 2 == 0, nxt, prv)
    reps = n // LANES
    cosf = jnp.concatenate([cos] * reps, axis=1)
    sinf = jnp.concatenate([sin_signed] * reps, axis=1)
    return t * cosf + partner * sinf


def _repeat_heads(t):
    lane = lax.broadcasted_iota(jnp.int32, (t.shape[0], LANES), 1)
    low = lane < HEAD_DIM
    cols = []
    for j in range(KV_DIM // LANES):
        c = t[:, j * LANES:(j + 1) * LANES]
        sw = pltpu.roll(c, HEAD_DIM, 1)
        first = jnp.where(low, c, sw)
        second = jnp.where(low, sw, c)
        cols += [first] * (GROUP_W // LANES) + [second] * (GROUP_W // LANES)
    return jnp.concatenate(cols, axis=1)


def _qkv_kernel(x_ref, mod_ref, gain_ref, w_ref, qg_ref, kg_ref, cos_ref, sin_ref, bd_ref,
                q_ref, k_ref, vt_ref):
    mod = mod_ref[...]
    h = _modnorm(x_ref[...], gain_ref[...], mod[3:4], mod[4:5]).astype(BF16)
    qkv = _dot(h, w_ref[...])
    q = qkv[:, :Q_DIM]
    k = qkv[:, Q_DIM:Q_DIM + KV_DIM]
    v = qkv[:, Q_DIM + KV_DIM:]
    cos = cos_ref[...]
    sin = sin_ref[...]
    q = _rope(_head_rms(q, bd_ref[...], qg_ref[...]), cos, sin)
    k = _rope(_head_rms(k, bd_ref[:KV_DIM, :KV_DIM], kg_ref[...]), cos, sin)
    q_ref[...] = (q * (LOG2E / math.sqrt(HEAD_DIM))).astype(BF16)
    k_ref[...] = _repeat_heads(k).astype(BF16)
    v_t = v.T
    ones = jnp.ones((VT_ROWS - HEAD_DIM, v.shape[0]), BF16)
    for j in range(N_KV_HEADS):
        vt_ref[j, :HEAD_DIM, :] = v_t[j * HEAD_DIM:(j + 1) * HEAD_DIM].astype(BF16)
        vt_ref[j, HEAD_DIM:, :] = ones


def _rope_tables(s):
    rows = s // GRID_W
    r = jnp.repeat(jnp.arange(rows), GRID_W).astype(F32)
    c = jnp.tile(jnp.arange(GRID_W), rows).astype(F32)
    inv = ROPE_THETA ** (-jnp.arange(0, ROPE_AXIS_DIM, 2, dtype=F32) / ROPE_AXIS_DIM)
    ang = jnp.concatenate([r[:, None] * inv, c[:, None] * inv], axis=-1)
    cos = jnp.repeat(jnp.cos(ang), 2, axis=-1)
    sin = jnp.repeat(jnp.sin(ang), 2, axis=-1)
    sign = jnp.tile(jnp.array([-1.0, 1.0], F32), HEAD_DIM // 2)
    reps = LANES // HEAD_DIM
    return jnp.tile(cos, (1, reps)), jnp.tile(sin * sign, (1, reps))


def _qkv(x, mod, gain, w_qkv, q_gain, k_gain, cos, sin, ones_bd, layer, a):
    b, s, _ = x.shape
    tm = min(TOK_TM, s)
    tok = lambda i, t: (i, t, 0)
    const2 = lambda i, t: (0, 0)
    shape = jax.ShapeDtypeStruct((b, s, D_MODEL), BF16)
    tk = min(ATTN_TK, s)
    per = tk // tm
    vt_shape = jax.ShapeDtypeStruct((b, N_KV_HEADS, s // tk, VT_ROWS, tk), BF16)
    vt_spec = pl.BlockSpec((None, N_KV_HEADS, None, VT_ROWS, tm), lambda i, t: (i, 0, t // per, 0, t % per))
    return pl.pallas_call(
        _qkv_kernel,
        grid=(b, s // tm),
        in_specs=[
            pl.BlockSpec((None, tm, D_MODEL), tok),
            pl.BlockSpec((None, None, N_SUBLAYERS * N_MOD, D_MODEL), lambda i, t: (layer, i, 0, 0)),
            pl.BlockSpec((1, D_MODEL), const2),
            pl.BlockSpec((None, D_MODEL, QKV_DIM), lambda i, t: (a, 0, 0)),
            pl.BlockSpec((1, Q_DIM), const2),
            pl.BlockSpec((1, KV_DIM), const2),
            pl.BlockSpec((tm, LANES), lambda i, t: (t, 0)),
            pl.BlockSpec((tm, LANES), lambda i, t: (t, 0)),
            pl.BlockSpec((Q_DIM, Q_DIM), const2),
        ],
        out_specs=[pl.BlockSpec((None, tm, D_MODEL), tok)] * 2 + [vt_spec],
        out_shape=[shape, shape, vt_shape],
        compiler_params=_params("parallel", "parallel"),
        name=f"qkv_l{layer}",
    )(x, mod, gain, w_qkv, q_gain, k_gain, cos, sin, ones_bd)


def _attn_kernel(q_ref, k_ref, vt_ref, o_ref, *, tq, tk, nk):
    q = q_ref[...]
    head = lax.broadcasted_iota(jnp.int32, q.shape, 1) // HEAD_DIM
    zero = jnp.zeros_like(q)
    qs = jnp.concatenate([jnp.where(head == g, q, zero) for g in range(KV_GROUP)], axis=0)
    w = KV_GROUP * tq

    def chunk(c, carry):
        m, acc = carry
        kc = k_ref[pl.ds(pl.multiple_of(c * tk, tk), tk), :]
        st = lax.dot_general(kc, qs, (((1,), (1,)), ((), ())), preferred_element_type=F32)
        m_new = jnp.maximum(m, jnp.max(st, axis=0, keepdims=True))
        p = jnp.exp2(st - m_new).astype(BF16)
        return m_new, jnp.exp2(m - m_new) * acc + _dot(vt_ref[c], p)

    init = (jnp.full((1, w), -1e30, F32), jnp.zeros((VT_ROWS, w), F32))
    if nk == 1:
        _, acc = chunk(0, init)
    else:
        _, acc = lax.fori_loop(0, nk, chunk, init)
    out = acc[:HEAD_DIM] / acc[HEAD_DIM:HEAD_DIM + 1]
    o_t = jnp.concatenate([out[:, g * tq:(g + 1) * tq] for g in range(KV_GROUP)], axis=0)
    o_ref[...] = o_t.T.astype(BF16)


def _attention(q, k4, vt, layer):
    b, s, _ = q.shape
    nk, tk = vt.shape[2], vt.shape[4]
    tq = min(ATTN_TQ, s)
    return pl.pallas_call(
        functools.partial(_attn_kernel, tq=tq, tk=tk, nk=nk),
        grid=(b, N_KV_HEADS, s // tq),
        in_specs=[
            pl.BlockSpec((None, tq, GROUP_W), lambda i, j, t: (i, t, j)),
            pl.BlockSpec((None, s, GROUP_W), lambda i, j, t: (i, 0, j)),
            pl.BlockSpec((None, None, nk, VT_ROWS, tk), lambda i, j, t: (i, j, 0, 0, 0)),
        ],
        out_specs=pl.BlockSpec((None, tq, GROUP_W), lambda i, j, t: (i, t, j)),
        out_shape=jax.ShapeDtypeStruct((b, s, D_MODEL), BF16),
        compiler_params=_params("parallel", "parallel", "parallel"),
        name=f"attn_l{layer}",
    )(q, k4, vt)


def _proj_kernel(o_ref, x_ref, mod_ref, w_ref, out_ref):
    gate = mod_ref[...][5:6]
    out_ref[...] = x_ref[...] + gate * _dot(o_ref[...], w_ref[...])


def _proj_residual(o, x, mod, w_o, layer, a):
    b, s, _ = x.shape
    tm = min(TOK_TM, s)
    tok = lambda i, t: (i, t, 0)
    return pl.pallas_call(
        _proj_kernel,
        grid=(b, s // tm),
        in_specs=[
            pl.BlockSpec((None, tm, D_MODEL), tok),
            pl.BlockSpec((None, tm, D_MODEL), tok),
            pl.BlockSpec((None, None, N_SUBLAYERS * N_MOD, D_MODEL), lambda i, t: (layer, i, 0, 0)),
            pl.BlockSpec((None, D_MODEL, D_MODEL), lambda i, t: (a, 0, 0)),
        ],
        out_specs=pl.BlockSpec((None, tm, D_MODEL), tok),
        out_shape=jax.ShapeDtypeStruct(x.shape, F32),
        compiler_params=_params("parallel", "parallel"),
        name=f"attn_out_l{layer}",
    )(o, x, mod, w_o)


def _dft_tables(s):
    n2 = DFT_N2
    n1 = s // n2
    k1 = np.arange(n1)
    ang1 = 2.0 * np.pi * ((k1[:, None] * k1[None, :]) % n1) / n1
    m1 = np.concatenate([np.cos(ang1), -np.sin(ang1)], axis=0)
    s2 = np.arange(n2)
    num = (s2[None, None, :] * k1[:, None, None] + s2[None, None, :] * s2[None, :, None] * n1) % s
    psi = 2.0 * np.pi * num / s
    tr, ti = np.cos(psi), -np.sin(psi)
    t = np.concatenate([np.concatenate([tr, -ti], axis=2), np.concatenate([ti, tr], axis=2)], axis=1)
    c = np.arange(FNET_GROUP_DIM)
    phi = 2.0 * np.pi * ((c[:, None] * c[None, :]) % FNET_GROUP_DIM) / FNET_GROUP_DIM
    cs = np.concatenate([np.cos(phi), np.sin(phi)], axis=0)
    return jnp.asarray(m1, BF16), jnp.asarray(t, BF16), jnp.asarray(cs, BF16)


def _modnorm_kernel(x_ref, mod_ref, gain_ref, h_ref):
    mod = mod_ref[...]
    h_ref[...] = _modnorm(x_ref[...], gain_ref[...], mod[3:4], mod[4:5]).astype(BF16)


def _mixer_norm(x, mod, gain, layer):
    b, s, _ = x.shape
    tm = min(TOK_TM, s)
    tok = lambda i, t: (i, t, 0)
    return pl.pallas_call(
        _modnorm_kernel,
        grid=(b, s // tm),
        in_specs=[
            pl.BlockSpec((None, tm, D_MODEL), tok),
            pl.BlockSpec((None, None, N_SUBLAYERS * N_MOD, D_MODEL), lambda i, t: (layer, i, 0, 0)),
            pl.BlockSpec((1, D_MODEL), lambda i, t: (0, 0)),
        ],
        out_specs=pl.BlockSpec((None, tm, D_MODEL), tok),
        out_shape=jax.ShapeDtypeStruct(x.shape, BF16),
        compiler_params=_params("parallel", "parallel"),
        name=f"fnet_norm_l{layer}",
    )(x, mod, gain)


def _dft1_kernel(m_ref, h_ref, y_ref):
    y_ref[...] = _dot(m_ref[...], h_ref[...]).astype(BF16)


def _dft1(h, m1, layer):
    b, n1, cols = h.shape
    tc = min(DFT1_TC, cols)
    return pl.pallas_call(
        _dft1_kernel,
        grid=(b, cols // tc),
        in_specs=[
            pl.BlockSpec((2 * n1, n1), lambda i, t: (0, 0)),
            pl.BlockSpec((None, n1, tc), lambda i, t: (i, 0, t)),
        ],
        out_specs=pl.BlockSpec((None, 2 * n1, tc), lambda i, t: (i, 0, t)),
        out_shape=jax.ShapeDtypeStruct((b, 2 * n1, cols), BF16),
        compiler_params=_params("parallel", "parallel"),
        name=f"fnet_dft1_l{layer}",
    )(m1, h)


def _dft2_kernel(t_ref, y_ref, u_ref, *, kb, n2):
    for i in range(kb):
        y = jnp.concatenate([y_ref[0, i], y_ref[1, i]], axis=0)
        u = _dot(t_ref[i], y)
        u_ref[0, :, i * D_MODEL:(i + 1) * D_MODEL] = u[:n2].astype(BF16)
        u_ref[1, :, i * D_MODEL:(i + 1) * D_MODEL] = u[n2:].astype(BF16)


def _dft2(y, t, layer):
    b, _, n1, n2, _ = y.shape
    kb = min(DFT_KB, n1)
    return pl.pallas_call(
        functools.partial(_dft2_kernel, kb=kb, n2=n2),
        grid=(b, n1 // kb),
        in_specs=[
            pl.BlockSpec((kb, 2 * n2, 2 * n2), lambda i, t: (t, 0, 0)),
            pl.BlockSpec((None, 2, kb, n2, D_MODEL), lambda i, t: (i, 0, t, 0, 0)),
        ],
        out_specs=pl.BlockSpec((None, 2, n2, kb * D_MODEL), lambda i, t: (i, 0, 0, t)),
        out_shape=jax.ShapeDtypeStruct((b, 2, n2, n1 * D_MODEL), BF16),
        compiler_params=_params("parallel", "parallel"),
        name=f"fnet_dft2_l{layer}",
    )(t, y)


def _dft3_kernel(u_ref, x_ref, mod_ref, cs_ref, w_ref, out_ref, *, inv_norm):
    ur = u_ref[0]
    ui = u_ref[1]
    cs = cs_ref[...]
    parts = []
    for g in range(FNET_GROUPS):
        cols = slice(g * FNET_GROUP_DIM, (g + 1) * FNET_GROUP_DIM)
        parts.append(_dot(jnp.concatenate([ur[:, cols], ui[:, cols]], axis=1), cs))
    f = (jnp.concatenate(parts, axis=1) * inv_norm).astype(BF16)
    gate = mod_ref[...][5:6]
    out_ref[...] = x_ref[...] + gate * _dot(f, w_ref[...])


def _dft3_residual(u, x, mod, cs, w_o, layer, a):
    b, s, _ = x.shape
    tm = min(TOK_TM, s)
    tok = lambda i, t: (i, t, 0)
    inv_norm = 1.0 / math.sqrt(s * FNET_GROUP_DIM)
    return pl.pallas_call(
        functools.partial(_dft3_kernel, inv_norm=inv_norm),
        grid=(b, s // tm),
        in_specs=[
            pl.BlockSpec((None, 2, tm, D_MODEL), lambda i, t: (i, 0, t, 0)),
            pl.BlockSpec((None, tm, D_MODEL), tok),
            pl.BlockSpec((None, None, N_SUBLAYERS * N_MOD, D_MODEL), lambda i, t: (layer, i, 0, 0)),
            pl.BlockSpec((2 * FNET_GROUP_DIM, FNET_GROUP_DIM), lambda i, t: (0, 0)),
            pl.BlockSpec((None, D_MODEL, D_MODEL), lambda i, t: (a, 0, 0)),
        ],
        out_specs=pl.BlockSpec((None, tm, D_MODEL), tok),
        out_shape=jax.ShapeDtypeStruct(x.shape, F32),
        compiler_params=_params("parallel", "parallel"),
        name=f"fnet_out_l{layer}",
    )(u, x, mod, cs, w_o)


def _fourier_mix(x, mod, gain, w_o, tables, layer, a):
    b, s, _ = x.shape
    m1, t, cs = tables
    n2 = DFT_N2
    n1 = s // n2
    h = _mixer_norm(x, mod, gain, layer)
    y = _dft1(h.reshape(b, n1, n2 * D_MODEL), m1, layer)
    u = _dft2(y.reshape(b, 2, n1, n2, D_MODEL), t, layer)
    return _dft3_residual(u.reshape(b, 2, s, D_MODEL), x, mod, cs, w_o, layer, a)


def _run_trunk(x, mod, w):
    s = x.shape[1]
    cos, sin = _rope_tables(s)
    tables = _dft_tables(s)
    for l in range(DEPTH):
        gains = w["norm_gain"][l]
        x = _ffn(x, mod, gains[0:1], w["ffn_w_in"], w["ffn_w_out"], l, 0, 0)
        a = l // 2
        if l % 2 == 0:
            q, k4, vt = _qkv(x, mod, gains[1:2], w["attn_w_qkv"], w["q_gain"][a], w["k_gain"][a],
                             cos, sin, w["ones_bd"], l, a)
            o = _attention(q, k4, vt, l)
            x = _proj_residual(o, x, mod, w["attn_w_o"], l, a)
        else:
            x = _fourier_mix(x, mod, gains[1:2], w["fnet_w_o"], tables, l, a)
        x = _ffn(x, mod, gains[2:3], w["ffn_w_in"], w["ffn_w_out"], l, 1, 2)
    return x


def kernel(x_prompt, x_sample, c_prompt, c_sample, norm_gain, ada_w, ada_b, ffn_w_in, ffn_w_out,
           attn_w_qkv, attn_q_gain, attn_k_gain, attn_w_o, fnet_w_o):
    head = np.arange(Q_DIM) // HEAD_DIM
    w = {
        "norm_gain": norm_gain,
        "ffn_w_in": ffn_w_in.astype(BF16),
        "ffn_w_out": ffn_w_out.astype(BF16),
        "attn_w_qkv": attn_w_qkv.astype(BF16),
        "attn_w_o": attn_w_o.astype(BF16),
        "fnet_w_o": fnet_w_o.astype(BF16),
        "q_gain": jnp.tile(attn_q_gain, (1, N_HEADS))[:, None, :],
        "k_gain": jnp.tile(attn_k_gain, (1, N_KV_HEADS))[:, None, :],
        "ones_bd": jnp.asarray(head[:, None] == head[None, :], BF16),
    }
    nb = x_prompt.shape[0]
    mod = _modulation(jnp.concatenate([c_prompt, c_sample], axis=0), ada_w, ada_b)
    y_prompt = _run_trunk(x_prompt, mod[:, :nb], w)
    y_sample = _run_trunk(x_sample, mod[:, nb:], w)
    return (y_prompt, y_sample)
```

```python
import functools
import math

import numpy as np
import jax
import jax.numpy as jnp
from jax import lax
from jax.experimental import pallas as pl
from jax.experimental.pallas import tpu as pltpu

F32 = jnp.float32
BF16 = jnp.bfloat16

D_MODEL = 1024
DEPTH = 4
N_HEADS = 16
N_KV_HEADS = 4
KV_GROUP = N_HEADS // N_KV_HEADS
HEAD_DIM = D_MODEL // N_HEADS
Q_DIM = N_HEADS * HEAD_DIM
KV_DIM = N_KV_HEADS * HEAD_DIM
QKV_DIM = Q_DIM + 2 * KV_DIM
GROUP_W = KV_GROUP * HEAD_DIM
ROPE_AXIS_DIM = HEAD_DIM // 2
ROPE_THETA = 10000.0
FNET_GROUPS = 8
FNET_GROUP_DIM = D_MODEL // FNET_GROUPS
D_FF = 2816
GRID_W = 64
N_SUBLAYERS = 3
N_MOD = 3
EPS = 1e-6
LOG2E = 1.4426950408889634

LANES = 128
VMEM_LIMIT = 48 * 1024 * 1024

FFN_TM = 512
TOK_TM = 512
ATTN_TQ = 256
ATTN_TK = 2048
ATTN_NOSHIFT_MAX_SCORE = 48.0
SCORE_BOUND_SLACK = 1.02
VT_ROWS = HEAD_DIM + 16
DFT_N2 = 64
DFT_KB = 8
DFT1_TC = 4096


def _params(*sem):
    return pltpu.CompilerParams(dimension_semantics=sem, vmem_limit_bytes=VMEM_LIMIT)


def _dot(a, b):
    return jnp.dot(a, b, preferred_element_type=F32)


def _modnorm(x, gain, shift, scale):
    ms = jnp.mean(x * x, axis=-1, keepdims=True)
    return (x * lax.rsqrt(ms + EPS) * gain) * (1.0 + scale) + shift


def _mod_kernel(c_ref, w_ref, b_ref, o_ref):
    c = c_ref[...]
    a = (c * jax.nn.sigmoid(c)).astype(BF16)
    o_ref[...] = _dot(a, w_ref[...].astype(BF16)) + b_ref[...]


def _modulation(c, ada_w, ada_b):
    b = c.shape[0]
    n = N_SUBLAYERS * N_MOD
    out = pl.pallas_call(
        _mod_kernel,
        grid=(DEPTH, n),
        in_specs=[
            pl.BlockSpec((b, D_MODEL), lambda l, j: (0, 0)),
            pl.BlockSpec((None, D_MODEL, D_MODEL), lambda l, j: (l, 0, j)),
            pl.BlockSpec((None, 1, D_MODEL), lambda l, j: (l, 0, j)),
        ],
        out_specs=pl.BlockSpec((None, b, D_MODEL), lambda l, j: (l, 0, j)),
        out_shape=jax.ShapeDtypeStruct((DEPTH, b, n * D_MODEL), F32),
        compiler_params=_params("parallel", "parallel"),
        name="adaln_mod",
    )(c, ada_w, ada_b.reshape(DEPTH, 1, n * D_MODEL))
    return out.reshape(DEPTH, b, n, D_MODEL)


def _ffn_kernel(x_ref, mod_ref, gain_ref, wg_ref, wu_ref, wo_ref, o_ref, *, sub):
    x = x_ref[...]
    mod = mod_ref[...]
    h = _modnorm(x, gain_ref[...], mod[3 * sub:3 * sub + 1], mod[3 * sub + 1:3 * sub + 2]).astype(BF16)
    g = _dot(h, wg_ref[...])
    u = _dot(h, wu_ref[...])
    a = (g * jax.nn.sigmoid(g) * u).astype(BF16)
    o_ref[...] = x + (0.5 * mod[3 * sub + 2:3 * sub + 3]) * _dot(a, wo_ref[...])


def _resident(block_shape, index_map):
    return pl.BlockSpec(block_shape, index_map, pipeline_mode=pl.Buffered(1))


def _ffn(x, mod, gain, w_in, w_out, layer, which, sub):
    b, s, _ = x.shape
    tm = min(FFN_TM, s)
    return pl.pallas_call(
        functools.partial(_ffn_kernel, sub=sub),
        grid=(b, s // tm),
        in_specs=[
            pl.BlockSpec((None, tm, D_MODEL), lambda i, t: (i, t, 0)),
            pl.BlockSpec((None, None, N_SUBLAYERS * N_MOD, D_MODEL), lambda i, t: (layer, i, 0, 0)),
            pl.BlockSpec((1, D_MODEL), lambda i, t: (0, 0)),
            _resident((None, None, D_MODEL, D_FF), lambda i, t: (layer, which, 0, 0)),
            _resident((None, None, D_MODEL, D_FF), lambda i, t: (layer, which, 0, 1)),
            _resident((None, None, D_FF, D_MODEL), lambda i, t: (layer, which, 0, 0)),
        ],
        out_specs=pl.BlockSpec((None, tm, D_MODEL), lambda i, t: (i, t, 0)),
        out_shape=jax.ShapeDtypeStruct(x.shape, F32),
        compiler_params=_params("parallel", "parallel"),
        name=f"ffn_l{layer}_{which}",
    )(x, mod, gain, w_in, w_in, w_out)


def _head_rms(t, ones_bd, gain):
    t2 = t * t
    hi = t2.astype(BF16)
    lo = (t2 - hi.astype(F32)).astype(BF16)
    ss = _dot(hi, ones_bd) + _dot(lo, ones_bd)
    return t * lax.rsqrt(ss * (1.0 / HEAD_DIM) + EPS) * gain


def _rope(t, cos, sin_signed):
    n = t.shape[1]
    lane = lax.broadcasted_iota(jnp.int32, t.shape, 1)
    nxt = pltpu.roll(t, n - 1, 1)
    prv = pltpu.roll(t, 1, 1)
    partner = jnp.where(lane % 2 == 0, nxt, prv)
    reps = n // LANES
    cosf = jnp.concatenate([cos] * reps, axis=1)
    sinf = jnp.concatenate([sin_signed] * reps, axis=1)
    return t * cosf + partner * sinf


def _repeat_heads(t):
    lane = lax.broadcasted_iota(jnp.int32, (t.shape[0], LANES), 1)
    low = lane < HEAD_DIM
    cols = []
    for j in range(KV_DIM // LANES):
        c = t[:, j * LANES:(j + 1) * LANES]
        sw = pltpu.roll(c, HEAD_DIM, 1)
        first = jnp.where(low, c, sw)
        second = jnp.where(low, sw, c)
        cols += [first] * (GROUP_W // LANES) + [second] * (GROUP_W // LANES)
    return jnp.concatenate(cols, axis=1)


def _qkv_kernel(x_ref, mod_ref, gain_ref, w_ref, qg_ref, kg_ref, cos_ref, sin_ref, bd_ref,
                q_ref, k_ref, vt_ref):
    mod = mod_ref[...]
    h = _modnorm(x_ref[...], gain_ref[...], mod[3:4], mod[4:5]).astype(BF16)
    qkv = _dot(h, w_ref[...])
    q = qkv[:, :Q_DIM]
    k = qkv[:, Q_DIM:Q_DIM + KV_DIM]
    v = qkv[:, Q_DIM + KV_DIM:]
    cos = cos_ref[...]
    sin = sin_ref[...]
    q = _rope(_head_rms(q, bd_ref[...], qg_ref[...]), cos, sin)
    k = _rope(_head_rms(k, bd_ref[:KV_DIM, :KV_DIM], kg_ref[...]), cos, sin)
    q_ref[...] = (q * (LOG2E / math.sqrt(HEAD_DIM))).astype(BF16)
    k_ref[...] = _repeat_heads(k).astype(BF16)
    v_t = v.T
    ones = jnp.ones((VT_ROWS - HEAD_DIM, v.shape[0]), BF16)
    for j in range(N_KV_HEADS):
        vt_ref[j, :HEAD_DIM, :] = v_t[j * HEAD_DIM:(j + 1) * HEAD_DIM].astype(BF16)
        vt_ref[j, HEAD_DIM:, :] = ones


def _rope_tables(s):
    rows = s // GRID_W
    r = jnp.repeat(jnp.arange(rows), GRID_W).astype(F32)
    c = jnp.tile(jnp.arange(GRID_W), rows).astype(F32)
    inv = ROPE_THETA ** (-jnp.arange(0, ROPE_AXIS_DIM, 2, dtype=F32) / ROPE_AXIS_DIM)
    ang = jnp.concatenate([r[:, None] * inv, c[:, None] * inv], axis=-1)
    cos = jnp.repeat(jnp.cos(ang), 2, axis=-1)
    sin = jnp.repeat(jnp.sin(ang), 2, axis=-1)
    sign = jnp.tile(jnp.array([-1.0, 1.0], F32), HEAD_DIM // 2)
    reps = LANES // HEAD_DIM
    return jnp.tile(cos, (1, reps)), jnp.tile(sin * sign, (1, reps))


def _qkv(x, mod, gain, w_qkv, q_gain, k_gain, cos, sin, ones_bd, layer, a):
    b, s, _ = x.shape
    tm = min(TOK_TM, s)
    tok = lambda i, t: (i, t, 0)
    const2 = lambda i, t: (0, 0)
    shape = jax.ShapeDtypeStruct((b, s, D_MODEL), BF16)
    tk = min(ATTN_TK, s)
    per = tk // tm
    vt_shape = jax.ShapeDtypeStruct((b, N_KV_HEADS, s // tk, VT_ROWS, tk), BF16)
    vt_spec = pl.BlockSpec((None, N_KV_HEADS, None, VT_ROWS, tm), lambda i, t: (i, 0, t // per, 0, t % per))
    return pl.pallas_call(
        _qkv_kernel,
        grid=(b, s // tm),
        in_specs=[
            pl.BlockSpec((None, tm, D_MODEL), tok),
            pl.BlockSpec((None, None, N_SUBLAYERS * N_MOD, D_MODEL), lambda i, t: (layer, i, 0, 0)),
            pl.BlockSpec((1, D_MODEL), const2),
            pl.BlockSpec((None, D_MODEL, QKV_DIM), lambda i, t: (a, 0, 0)),
            pl.BlockSpec((1, Q_DIM), const2),
            pl.BlockSpec((1, KV_DIM), const2),
            pl.BlockSpec((tm, LANES), lambda i, t: (t, 0)),
            pl.BlockSpec((tm, LANES), lambda i, t: (t, 0)),
            pl.BlockSpec((Q_DIM, Q_DIM), const2),
        ],
        out_specs=[pl.BlockSpec((None, tm, D_MODEL), tok)] * 2 + [vt_spec],
        out_shape=[shape, shape, vt_shape],
        compiler_params=_params("parallel", "parallel"),
        name=f"qkv_l{layer}",
    )(x, mod, gain, w_qkv, q_gain, k_gain, cos, sin, ones_bd)


def _attn_kernel(q_ref, k_ref, vt_ref, o_ref, *, tq, tk, nk, shift):
    q = q_ref[...]
    head = lax.broadcasted_iota(jnp.int32, q.shape, 1) // HEAD_DIM
    zero = jnp.zeros_like(q)
    qs = jnp.concatenate([jnp.where(head == g, q, zero) for g in range(KV_GROUP)], axis=0)
    w = KV_GROUP * tq

    def scores(c):
        kc = k_ref[pl.ds(pl.multiple_of(c * tk, tk), tk), :]
        return lax.dot_general(kc, qs, (((1,), (1,)), ((), ())), preferred_element_type=F32)

    def chunk_shifted(c, carry):
        m, acc = carry
        st = scores(c)
        m_new = jnp.maximum(m, jnp.max(st, axis=0, keepdims=True))
        p = jnp.exp2(st - m_new).astype(BF16)
        return m_new, jnp.exp2(m - m_new) * acc + _dot(vt_ref[c], p)

    def chunk_plain(c, acc):
        return acc + _dot(vt_ref[c], jnp.exp2(scores(c)).astype(BF16))

    acc0 = jnp.zeros((VT_ROWS, w), F32)
    if shift:
        _, acc = lax.fori_loop(0, nk, chunk_shifted, (jnp.full((1, w), -1e30, F32), acc0))
    else:
        acc = lax.fori_loop(0, nk, chunk_plain, acc0)
    out = acc[:HEAD_DIM] / acc[HEAD_DIM:HEAD_DIM + 1]
    o_t = jnp.concatenate([out[:, g * tq:(g + 1) * tq] for g in range(KV_GROUP)], axis=0)
    o_ref[...] = o_t.T.astype(BF16)


def _attention(q, k4, vt, score_bound, layer):
    b, s, _ = q.shape
    nk, tk = vt.shape[2], vt.shape[4]
    tq = min(ATTN_TQ, s)

    def call(shift):
        return pl.pallas_call(
            functools.partial(_attn_kernel, tq=tq, tk=tk, nk=nk, shift=shift),
            grid=(b, N_KV_HEADS, s // tq),
            in_specs=[
                pl.BlockSpec((None, tq, GROUP_W), lambda i, j, t: (i, t, j)),
                pl.BlockSpec((None, s, GROUP_W), lambda i, j, t: (i, 0, j)),
                pl.BlockSpec((None, None, nk, VT_ROWS, tk), lambda i, j, t: (i, j, 0, 0, 0)),
            ],
            out_specs=pl.BlockSpec((None, tq, GROUP_W), lambda i, j, t: (i, t, j)),
            out_shape=jax.ShapeDtypeStruct((b, s, D_MODEL), BF16),
            compiler_params=_params("parallel", "parallel", "parallel"),
            name=f"attn_l{layer}_{'shifted' if shift else 'plain'}",
        )

    return lax.cond(score_bound <= ATTN_NOSHIFT_MAX_SCORE,
                    lambda *a: call(False)(*a), lambda *a: call(True)(*a), q, k4, vt)


def _proj_kernel(o_ref, x_ref, mod_ref, w_ref, out_ref):
    gate = mod_ref[...][5:6]
    out_ref[...] = x_ref[...] + gate * _dot(o_ref[...], w_ref[...])


def _proj_residual(o, x, mod, w_o, layer, a):
    b, s, _ = x.shape
    tm = min(TOK_TM, s)
    tok = lambda i, t: (i, t, 0)
    return pl.pallas_call(
        _proj_kernel,
        grid=(b, s // tm),
        in_specs=[
            pl.BlockSpec((None, tm, D_MODEL), tok),
            pl.BlockSpec((None, tm, D_MODEL), tok),
            pl.BlockSpec((None, None, N_SUBLAYERS * N_MOD, D_MODEL), lambda i, t: (layer, i, 0, 0)),
            pl.BlockSpec((None, D_MODEL, D_MODEL), lambda i, t: (a, 0, 0)),
        ],
        out_specs=pl.BlockSpec((None, tm, D_MODEL), tok),
        out_shape=jax.ShapeDtypeStruct(x.shape, F32),
        compiler_params=_params("parallel", "parallel"),
        name=f"attn_out_l{layer}",
    )(o, x, mod, w_o)


def _dft_tables(s):
    n2 = DFT_N2
    n1 = s // n2
    k1 = np.arange(n1)
    ang1 = 2.0 * np.pi * ((k1[:, None] * k1[None, :]) % n1) / n1
    m1 = np.concatenate([np.cos(ang1), -np.sin(ang1)], axis=0)
    s2 = np.arange(n2)
    num = (s2[None, None, :] * k1[:, None, None] + s2[None, None, :] * s2[None, :, None] * n1) % s
    psi = 2.0 * np.pi * num / s
    tr, ti = np.cos(psi), -np.sin(psi)
    t = np.concatenate([np.concatenate([tr, -ti], axis=2), np.concatenate([ti, tr], axis=2)], axis=1)
    c = np.arange(FNET_GROUP_DIM)
    phi = 2.0 * np.pi * ((c[:, None] * c[None, :]) % FNET_GROUP_DIM) / FNET_GROUP_DIM
    cs = np.concatenate([np.cos(phi), np.sin(phi)], axis=0)
    return jnp.asarray(m1, BF16), jnp.asarray(t, BF16), jnp.asarray(cs, BF16)


def _modnorm_kernel(x_ref, mod_ref, gain_ref, h_ref):
    mod = mod_ref[...]
    h_ref[...] = _modnorm(x_ref[...], gain_ref[...], mod[3:4], mod[4:5]).astype(BF16)


def _mixer_norm(x, mod, gain, layer):
    b, s, _ = x.shape
    tm = min(TOK_TM, s)
    tok = lambda i, t: (i, t, 0)
    return pl.pallas_call(
        _modnorm_kernel,
        grid=(b, s // tm),
        in_specs=[
            pl.BlockSpec((None, tm, D_MODEL), tok),
            pl.BlockSpec((None, None, N_SUBLAYERS * N_MOD, D_MODEL), lambda i, t: (layer, i, 0, 0)),
            pl.BlockSpec((1, D_MODEL), lambda i, t: (0, 0)),
        ],
        out_specs=pl.BlockSpec((None, tm, D_MODEL), tok),
        out_shape=jax.ShapeDtypeStruct(x.shape, BF16),
        compiler_params=_params("parallel", "parallel"),
        name=f"fnet_norm_l{layer}",
    )(x, mod, gain)


def _dft1_kernel(m_ref, h_ref, y_ref):
    y_ref[...] = _dot(m_ref[...], h_ref[...]).astype(BF16)


def _dft1(h, m1, layer):
    b, n1, cols = h.shape
    tc = min(DFT1_TC, cols)
    return pl.pallas_call(
        _dft1_kernel,
        grid=(b, cols // tc),
        in_specs=[
            pl.BlockSpec((2 * n1, n1), lambda i, t: (0, 0)),
            pl.BlockSpec((None, n1, tc), lambda i, t: (i, 0, t)),
        ],
        out_specs=pl.BlockSpec((None, 2 * n1, tc), lambda i, t: (i, 0, t)),
        out_shape=jax.ShapeDtypeStruct((b, 2 * n1, cols), BF16),
        compiler_params=_params("parallel", "parallel"),
        name=f"fnet_dft1_l{layer}",
    )(m1, h)


def _dft2_kernel(t_ref, y_ref, u_ref, *, kb, n2):
    for i in range(kb):
        y = jnp.concatenate([y_ref[0, i], y_ref[1, i]], axis=0)
        u = _dot(t_ref[i], y)
        u_ref[0, :, i * D_MODEL:(i + 1) * D_MODEL] = u[:n2].astype(BF16)
        u_ref[1, :, i * D_MODEL:(i + 1) * D_MODEL] = u[n2:].astype(BF16)


def _dft2(y, t, layer):
    b, _, n1, n2, _ = y.shape
    kb = min(DFT_KB, n1)
    return pl.pallas_call(
        functools.partial(_dft2_kernel, kb=kb, n2=n2),
        grid=(b, n1 // kb),
        in_specs=[
            pl.BlockSpec((kb, 2 * n2, 2 * n2), lambda i, t: (t, 0, 0)),
            pl.BlockSpec((None, 2, kb, n2, D_MODEL), lambda i, t: (i, 0, t, 0, 0)),
        ],
        out_specs=pl.BlockSpec((None, 2, n2, kb * D_MODEL), lambda i, t: (i, 0, 0, t)),
        out_shape=jax.ShapeDtypeStruct((b, 2, n2, n1 * D_MODEL), BF16),
        compiler_params=_params("parallel", "parallel"),
        name=f"fnet_dft2_l{layer}",
    )(t, y)


def _dft3_kernel(u_ref, x_ref, mod_ref, cs_ref, w_ref, out_ref, *, inv_norm):
    ur = u_ref[0]
    ui = u_ref[1]
    cs = cs_ref[...]
    parts = []
    for g in range(FNET_GROUPS):
        cols = slice(g * FNET_GROUP_DIM, (g + 1) * FNET_GROUP_DIM)
        parts.append(_dot(jnp.concatenate([ur[:, cols], ui[:, cols]], axis=1), cs))
    f = (jnp.concatenate(parts, axis=1) * inv_norm).astype(BF16)
    gate = mod_ref[...][5:6]
    out_ref[...] = x_ref[...] + gate * _dot(f, w_ref[...])


def _dft3_residual(u, x, mod, cs, w_o, layer, a):
    b, s, _ = x.shape
    tm = min(TOK_TM, s)
    tok = lambda i, t: (i, t, 0)
    inv_norm = 1.0 / math.sqrt(s * FNET_GROUP_DIM)
    return pl.pallas_call(
        functools.partial(_dft3_kernel, inv_norm=inv_norm),
        grid=(b, s // tm),
        in_specs=[
            pl.BlockSpec((None, 2, tm, D_MODEL), lambda i, t: (i, 0, t, 0)),
            pl.BlockSpec((None, tm, D_MODEL), tok),
            pl.BlockSpec((None, None, N_SUBLAYERS * N_MOD, D_MODEL), lambda i, t: (layer, i, 0, 0)),
            pl.BlockSpec((2 * FNET_GROUP_DIM, FNET_GROUP_DIM), lambda i, t: (0, 0)),
            pl.BlockSpec((None, D_MODEL, D_MODEL), lambda i, t: (a, 0, 0)),
        ],
        out_specs=pl.BlockSpec((None, tm, D_MODEL), tok),
        out_shape=jax.ShapeDtypeStruct(x.shape, F32),
        compiler_params=_params("parallel", "parallel"),
        name=f"fnet_out_l{layer}",
    )(u, x, mod, cs, w_o)


def _fourier_mix(x, mod, gain, w_o, tables, layer, a):
    b, s, _ = x.shape
    m1, t, cs = tables
    n2 = DFT_N2
    n1 = s // n2
    h = _mixer_norm(x, mod, gain, layer)
    y = _dft1(h.reshape(b, n1, n2 * D_MODEL), m1, layer)
    u = _dft2(y.reshape(b, 2, n1, n2, D_MODEL), t, layer)
    return _dft3_residual(u.reshape(b, 2, s, D_MODEL), x, mod, cs, w_o, layer, a)


def _run_trunk(x, mod, w):
    s = x.shape[1]
    cos, sin = _rope_tables(s)
    tables = _dft_tables(s)
    for l in range(DEPTH):
        gains = w["norm_gain"][l]
        x = _ffn(x, mod, gains[0:1], w["ffn_w_in"], w["ffn_w_out"], l, 0, 0)
        a = l // 2
        if l % 2 == 0:
            q, k4, vt = _qkv(x, mod, gains[1:2], w["attn_w_qkv"], w["q_gain"][a], w["k_gain"][a],
                             cos, sin, w["ones_bd"], l, a)
            o = _attention(q, k4, vt, w["score_bound"][a], l)
            x = _proj_residual(o, x, mod, w["attn_w_o"], l, a)
        else:
            x = _fourier_mix(x, mod, gains[1:2], w["fnet_w_o"], tables, l, a)
        x = _ffn(x, mod, gains[2:3], w["ffn_w_in"], w["ffn_w_out"], l, 1, 2)
    return x


def kernel(x_prompt, x_sample, c_prompt, c_sample, norm_gain, ada_w, ada_b, ffn_w_in, ffn_w_out,
           attn_w_qkv, attn_q_gain, attn_k_gain, attn_w_o, fnet_w_o):
    head = np.arange(Q_DIM) // HEAD_DIM
    w = {
        "norm_gain": norm_gain,
        "ffn_w_in": ffn_w_in.astype(BF16),
        "ffn_w_out": ffn_w_out.astype(BF16),
        "attn_w_qkv": attn_w_qkv.astype(BF16),
        "attn_w_o": attn_w_o.astype(BF16),
        "fnet_w_o": fnet_w_o.astype(BF16),
        "q_gain": jnp.tile(attn_q_gain, (1, N_HEADS))[:, None, :],
        "k_gain": jnp.tile(attn_k_gain, (1, N_KV_HEADS))[:, None, :],
        "ones_bd": jnp.asarray(head[:, None] == head[None, :], BF16),
        "score_bound": (SCORE_BOUND_SLACK * math.sqrt(HEAD_DIM) * LOG2E
                        * jnp.max(jnp.abs(attn_q_gain), axis=1) * jnp.max(jnp.abs(attn_k_gain), axis=1)),
    }
    nb = x_prompt.shape[0]
    mod = _modulation(jnp.concatenate([c_prompt, c_sample], axis=0), ada_w, ada_b)
    y_prompt = _run_trunk(x_prompt, mod[:, :nb], w)
    y_sample = _run_trunk(x_sample, mod[:, nb:], w)
    return (y_prompt, y_sample)
```

```python
import functools
import math

import numpy as np
import jax
import jax.numpy as jnp
from jax import lax
from jax.experimental import pallas as pl
from jax.experimental.pallas import tpu as pltpu

F32 = jnp.float32
BF16 = jnp.bfloat16

D_MODEL = 1024
DEPTH = 4
N_HEADS = 16
N_KV_HEADS = 4
KV_GROUP = N_HEADS // N_KV_HEADS
HEAD_DIM = D_MODEL // N_HEADS
Q_DIM = N_HEADS * HEAD_DIM
KV_DIM = N_KV_HEADS * HEAD_DIM
QKV_DIM = Q_DIM + 2 * KV_DIM
GROUP_W = KV_GROUP * HEAD_DIM
ROPE_AXIS_DIM = HEAD_DIM // 2
ROPE_THETA = 10000.0
FNET_GROUPS = 8
FNET_GROUP_DIM = D_MODEL // FNET_GROUPS
D_FF = 2816
GRID_W = 64
N_SUBLAYERS = 3
N_MOD = 3
EPS = 1e-6
LOG2E = 1.4426950408889634

LANES = 128
VMEM_LIMIT = 48 * 1024 * 1024

FFN_TM = 512
TOK_TM = 512
ATTN_TQ = 256
ATTN_TK = 2048
ATTN_NOSHIFT_MAX_SCORE = 48.0
SCORE_BOUND_SLACK = 1.02
VT_ROWS = HEAD_DIM + 16
DFT_N2_MIN = 64
DFT_N1_MAX = 64
DFT_KB = 8
DFT_SB = 16


def _params(*sem):
    return pltpu.CompilerParams(dimension_semantics=sem, vmem_limit_bytes=VMEM_LIMIT)


def _dot(a, b):
    return jnp.dot(a, b, preferred_element_type=F32)


def _modnorm(x, gain, shift, scale):
    ms = jnp.mean(x * x, axis=-1, keepdims=True)
    return (x * lax.rsqrt(ms + EPS) * gain) * (1.0 + scale) + shift


def _mod_kernel(c_ref, w_ref, b_ref, o_ref):
    c = c_ref[...]
    a = (c * jax.nn.sigmoid(c)).astype(BF16)
    o_ref[...] = _dot(a, w_ref[...].astype(BF16)) + b_ref[...]


def _modulation(c, ada_w, ada_b):
    b = c.shape[0]
    n = N_SUBLAYERS * N_MOD
    out = pl.pallas_call(
        _mod_kernel,
        grid=(DEPTH, n),
        in_specs=[
            pl.BlockSpec((b, D_MODEL), lambda l, j: (0, 0)),
            pl.BlockSpec((None, D_MODEL, D_MODEL), lambda l, j: (l, 0, j)),
            pl.BlockSpec((None, 1, D_MODEL), lambda l, j: (l, 0, j)),
        ],
        out_specs=pl.BlockSpec((None, b, D_MODEL), lambda l, j: (l, 0, j)),
        out_shape=jax.ShapeDtypeStruct((DEPTH, b, n * D_MODEL), F32),
        compiler_params=_params("parallel", "parallel"),
        name="adaln_mod",
    )(c, ada_w, ada_b.reshape(DEPTH, 1, n * D_MODEL))
    return out.reshape(DEPTH, b, n, D_MODEL)


def _ffn_kernel(x_ref, mod_ref, gain_ref, wg_ref, wu_ref, wo_ref, o_ref, *, sub):
    x = x_ref[...]
    mod = mod_ref[...]
    h = _modnorm(x, gain_ref[...], mod[3 * sub:3 * sub + 1], mod[3 * sub + 1:3 * sub + 2]).astype(BF16)
    g = _dot(h, wg_ref[...])
    u = _dot(h, wu_ref[...])
    a = (g * jax.nn.sigmoid(g) * u).astype(BF16)
    o_ref[...] = x + (0.5 * mod[3 * sub + 2:3 * sub + 3]) * _dot(a, wo_ref[...])


def _resident(block_shape, index_map):
    return pl.BlockSpec(block_shape, index_map, pipeline_mode=pl.Buffered(1))


def _ffn(x, mod, gain, w_in, w_out, layer, which, sub):
    b, s, _ = x.shape
    tm = min(FFN_TM, s)
    return pl.pallas_call(
        functools.partial(_ffn_kernel, sub=sub),
        grid=(b, s // tm),
        in_specs=[
            pl.BlockSpec((None, tm, D_MODEL), lambda i, t: (i, t, 0)),
            pl.BlockSpec((None, None, N_SUBLAYERS * N_MOD, D_MODEL), lambda i, t: (layer, i, 0, 0)),
            pl.BlockSpec((1, D_MODEL), lambda i, t: (0, 0)),
            _resident((None, None, D_MODEL, D_FF), lambda i, t: (layer, which, 0, 0)),
            _resident((None, None, D_MODEL, D_FF), lambda i, t: (layer, which, 0, 1)),
            _resident((None, None, D_FF, D_MODEL), lambda i, t: (layer, which, 0, 0)),
        ],
        out_specs=pl.BlockSpec((None, tm, D_MODEL), lambda i, t: (i, t, 0)),
        out_shape=jax.ShapeDtypeStruct(x.shape, F32),
        compiler_params=_params("parallel", "parallel"),
        name=f"ffn_l{layer}_{which}",
    )(x, mod, gain, w_in, w_in, w_out)


def _head_rms(t, ones_bd, gain):
    t2 = t * t
    hi = t2.astype(BF16)
    lo = (t2 - hi.astype(F32)).astype(BF16)
    ss = _dot(hi, ones_bd) + _dot(lo, ones_bd)
    return t * lax.rsqrt(ss * (1.0 / HEAD_DIM) + EPS) * gain


def _rope(t, cos, sin_signed):
    n = t.shape[1]
    lane = lax.broadcasted_iota(jnp.int32, t.shape, 1)
    nxt = pltpu.roll(t, n - 1, 1)
    prv = pltpu.roll(t, 1, 1)
    partner = jnp.where(lane % 2 == 0, nxt, prv)
    reps = n // LANES
    cosf = jnp.concatenate([cos] * reps, axis=1)
    sinf = jnp.concatenate([sin_signed] * reps, axis=1)
    return t * cosf + partner * sinf


def _repeat_heads(t):
    lane = lax.broadcasted_iota(jnp.int32, (t.shape[0], LANES), 1)
    low = lane < HEAD_DIM
    cols = []
    for j in range(KV_DIM // LANES):
        c = t[:, j * LANES:(j + 1) * LANES]
        sw = pltpu.roll(c, HEAD_DIM, 1)
        first = jnp.where(low, c, sw)
        second = jnp.where(low, sw, c)
        cols += [first] * (GROUP_W // LANES) + [second] * (GROUP_W // LANES)
    return jnp.concatenate(cols, axis=1)


def _qkv_kernel(x_ref, mod_ref, gain_ref, w_ref, qg_ref, kg_ref, cos_ref, sin_ref, bd_ref,
                q_ref, k_ref, vt_ref):
    mod = mod_ref[...]
    h = _modnorm(x_ref[...], gain_ref[...], mod[3:4], mod[4:5]).astype(BF16)
    qkv = _dot(h, w_ref[...])
    q = qkv[:, :Q_DIM]
    k = qkv[:, Q_DIM:Q_DIM + KV_DIM]
    v = qkv[:, Q_DIM + KV_DIM:]
    cos = cos_ref[...]
    sin = sin_ref[...]
    q = _rope(_head_rms(q, bd_ref[...], qg_ref[...]), cos, sin)
    k = _rope(_head_rms(k, bd_ref[:KV_DIM, :KV_DIM], kg_ref[...]), cos, sin)
    q_ref[...] = (q * (LOG2E / math.sqrt(HEAD_DIM))).astype(BF16)
    k_ref[...] = _repeat_heads(k).astype(BF16)
    v_t = v.T
    ones = jnp.ones((VT_ROWS - HEAD_DIM, v.shape[0]), BF16)
    for j in range(N_KV_HEADS):
        vt_ref[j, :HEAD_DIM, :] = v_t[j * HEAD_DIM:(j + 1) * HEAD_DIM].astype(BF16)
        vt_ref[j, HEAD_DIM:, :] = ones


def _rope_tables(s):
    rows = s // GRID_W
    r = jnp.repeat(jnp.arange(rows), GRID_W).astype(F32)
    c = jnp.tile(jnp.arange(GRID_W), rows).astype(F32)
    inv = ROPE_THETA ** (-jnp.arange(0, ROPE_AXIS_DIM, 2, dtype=F32) / ROPE_AXIS_DIM)
    ang = jnp.concatenate([r[:, None] * inv, c[:, None] * inv], axis=-1)
    cos = jnp.repeat(jnp.cos(ang), 2, axis=-1)
    sin = jnp.repeat(jnp.sin(ang), 2, axis=-1)
    sign = jnp.tile(jnp.array([-1.0, 1.0], F32), HEAD_DIM // 2)
    reps = LANES // HEAD_DIM
    return jnp.tile(cos, (1, reps)), jnp.tile(sin * sign, (1, reps))


def _qkv(x, mod, gain, w_qkv, q_gain, k_gain, cos, sin, ones_bd, layer, a):
    b, s, _ = x.shape
    tm = min(TOK_TM, s)
    tok = lambda i, t: (i, t, 0)
    const2 = lambda i, t: (0, 0)
    shape = jax.ShapeDtypeStruct((b, s, D_MODEL), BF16)
    tk = min(ATTN_TK, s)
    per = tk // tm
    vt_shape = jax.ShapeDtypeStruct((b, N_KV_HEADS, s // tk, VT_ROWS, tk), BF16)
    vt_spec = pl.BlockSpec((None, N_KV_HEADS, None, VT_ROWS, tm), lambda i, t: (i, 0, t // per, 0, t % per))
    return pl.pallas_call(
        _qkv_kernel,
        grid=(b, s // tm),
        in_specs=[
            pl.BlockSpec((None, tm, D_MODEL), tok),
            pl.BlockSpec((None, None, N_SUBLAYERS * N_MOD, D_MODEL), lambda i, t: (layer, i, 0, 0)),
            pl.BlockSpec((1, D_MODEL), const2),
            pl.BlockSpec((None, D_MODEL, QKV_DIM), lambda i, t: (a, 0, 0)),
            pl.BlockSpec((1, Q_DIM), const2),
            pl.BlockSpec((1, KV_DIM), const2),
            pl.BlockSpec((tm, LANES), lambda i, t: (t, 0)),
            pl.BlockSpec((tm, LANES), lambda i, t: (t, 0)),
            pl.BlockSpec((Q_DIM, Q_DIM), const2),
        ],
        out_specs=[pl.BlockSpec((None, tm, D_MODEL), tok)] * 2 + [vt_spec],
        out_shape=[shape, shape, vt_shape],
        compiler_params=_params("parallel", "parallel"),
        name=f"qkv_l{layer}",
    )(x, mod, gain, w_qkv, q_gain, k_gain, cos, sin, ones_bd)


def _attn_kernel(q_ref, k_ref, vt_ref, o_ref, *, tq, tk, nk, shift):
    q = q_ref[...]
    head = lax.broadcasted_iota(jnp.int32, q.shape, 1) // HEAD_DIM
    zero = jnp.zeros_like(q)
    qs = jnp.concatenate([jnp.where(head == g, q, zero) for g in range(KV_GROUP)], axis=0)
    w = KV_GROUP * tq

    def scores(c):
        kc = k_ref[pl.ds(pl.multiple_of(c * tk, tk), tk), :]
        return lax.dot_general(kc, qs, (((1,), (1,)), ((), ())), preferred_element_type=F32)

    def chunk_shifted(c, carry):
        m, acc = carry
        st = scores(c)
        m_new = jnp.maximum(m, jnp.max(st, axis=0, keepdims=True))
        p = jnp.exp2(st - m_new).astype(BF16)
        return m_new, jnp.exp2(m - m_new) * acc + _dot(vt_ref[c], p)

    def chunk_plain(c, acc):
        return acc + _dot(vt_ref[c], jnp.exp2(scores(c)).astype(BF16))

    acc0 = jnp.zeros((VT_ROWS, w), F32)
    if shift:
        _, acc = lax.fori_loop(0, nk, chunk_shifted, (jnp.full((1, w), -1e30, F32), acc0))
    else:
        acc = lax.fori_loop(0, nk, chunk_plain, acc0)
    out = acc[:HEAD_DIM] / acc[HEAD_DIM:HEAD_DIM + 1]
    o_t = jnp.concatenate([out[:, g * tq:(g + 1) * tq] for g in range(KV_GROUP)], axis=0)
    o_ref[...] = o_t.T.astype(BF16)


def _attention(q, k4, vt, score_bound, layer):
    b, s, _ = q.shape
    nk, tk = vt.shape[2], vt.shape[4]
    tq = min(ATTN_TQ, s)

    def call(shift):
        return pl.pallas_call(
            functools.partial(_attn_kernel, tq=tq, tk=tk, nk=nk, shift=shift),
            grid=(b, N_KV_HEADS, s // tq),
            in_specs=[
                pl.BlockSpec((None, tq, GROUP_W), lambda i, j, t: (i, t, j)),
                pl.BlockSpec((None, s, GROUP_W), lambda i, j, t: (i, 0, j)),
                pl.BlockSpec((None, None, nk, VT_ROWS, tk), lambda i, j, t: (i, j, 0, 0, 0)),
            ],
            out_specs=pl.BlockSpec((None, tq, GROUP_W), lambda i, j, t: (i, t, j)),
            out_shape=jax.ShapeDtypeStruct((b, s, D_MODEL), BF16),
            compiler_params=_params("parallel", "parallel", "parallel"),
            name=f"attn_l{layer}_{'shifted' if shift else 'plain'}",
        )

    return lax.cond(score_bound <= ATTN_NOSHIFT_MAX_SCORE,
                    lambda *a: call(False)(*a), lambda *a: call(True)(*a), q, k4, vt)


def _proj_kernel(o_ref, x_ref, mod_ref, w_ref, out_ref):
    gate = mod_ref[...][5:6]
    out_ref[...] = x_ref[...] + gate * _dot(o_ref[...], w_ref[...])


def _proj_residual(o, x, mod, w_o, layer, a):
    b, s, _ = x.shape
    tm = min(TOK_TM, s)
    tok = lambda i, t: (i, t, 0)
    return pl.pallas_call(
        _proj_kernel,
        grid=(b, s // tm),
        in_specs=[
            pl.BlockSpec((None, tm, D_MODEL), tok),
            pl.BlockSpec((None, tm, D_MODEL), tok),
            pl.BlockSpec((None, None, N_SUBLAYERS * N_MOD, D_MODEL), lambda i, t: (layer, i, 0, 0)),
            pl.BlockSpec((None, D_MODEL, D_MODEL), lambda i, t: (a, 0, 0)),
        ],
        out_specs=pl.BlockSpec((None, tm, D_MODEL), tok),
        out_shape=jax.ShapeDtypeStruct(x.shape, F32),
        compiler_params=_params("parallel", "parallel"),
        name=f"attn_out_l{layer}",
    )(o, x, mod, w_o)


def _dft_split(s):
    n2 = max(DFT_N2_MIN, s // DFT_N1_MAX)
    return s // n2, n2


def _dft_tables(s):
    n1, n2 = _dft_split(s)
    k1 = np.arange(n1)
    ang1 = 2.0 * np.pi * ((k1[:, None] * k1[None, :]) % n1) / n1
    m1 = np.concatenate([np.cos(ang1), -np.sin(ang1)], axis=0)
    m1 = np.kron(m1, np.eye(DFT_SB))
    src = (np.arange(DFT_KB)[None, :] * n2 + np.arange(n2)[:, None]).reshape(-1)
    perm = np.zeros((DFT_KB * n2, DFT_KB * n2))
    perm[np.arange(DFT_KB * n2), src] = 1.0
    s2 = np.arange(n2)
    num = (s2[None, None, :] * k1[:, None, None] + s2[None, None, :] * s2[None, :, None] * n1) % s
    psi = 2.0 * np.pi * num / s
    tr, ti = np.cos(psi), -np.sin(psi)
    t = np.concatenate([np.concatenate([tr, -ti], axis=2), np.concatenate([ti, tr], axis=2)], axis=1)
    c = np.arange(FNET_GROUP_DIM)
    phi = 2.0 * np.pi * ((c[:, None] * c[None, :]) % FNET_GROUP_DIM) / FNET_GROUP_DIM
    cs = np.concatenate([np.cos(phi), np.sin(phi)], axis=0)
    return jnp.asarray(m1, BF16), jnp.asarray(t, BF16), jnp.asarray(cs, BF16), jnp.asarray(perm, BF16)


def _dft1_kernel(x_ref, mod_ref, gain_ref, m_ref, y_ref, *, n1, sb):
    mod = mod_ref[...]
    x = x_ref[...].reshape(n1 * sb, D_MODEL)
    h = _modnorm(x, gain_ref[...], mod[3:4], mod[4:5]).astype(BF16)
    y_ref[...] = _dot(m_ref[...], h).astype(BF16).reshape(2, n1, sb, D_MODEL)


def _dft1(x, mod, gain, m1, layer):
    b, n1, n2, _ = x.shape
    sb = DFT_SB
    return pl.pallas_call(
        functools.partial(_dft1_kernel, n1=n1, sb=sb),
        grid=(b, n2 // sb),
        in_specs=[
            pl.BlockSpec((None, n1, sb, D_MODEL), lambda i, t: (i, 0, t, 0)),
            pl.BlockSpec((None, None, N_SUBLAYERS * N_MOD, D_MODEL), lambda i, t: (layer, i, 0, 0)),
            pl.BlockSpec((1, D_MODEL), lambda i, t: (0, 0)),
            _resident((2 * n1 * sb, n1 * sb), lambda i, t: (0, 0)),
        ],
        out_specs=pl.BlockSpec((None, 2, n1, sb, D_MODEL), lambda i, t: (i, 0, 0, t, 0)),
        out_shape=jax.ShapeDtypeStruct((b, 2, n1, n2, D_MODEL), BF16),
        compiler_params=_params("parallel", "parallel"),
        name=f"fnet_dft1_l{layer}",
    )(x, mod, gain, m1)


def _dft2_kernel(t_ref, y_ref, x_ref, mod_ref, cs_ref, perm_ref, w_ref, out_ref, u_ref, *, kb, n2, inv_norm):
    for i in range(kb):
        y = jnp.concatenate([y_ref[0, i], y_ref[1, i]], axis=0)
        u = _dot(t_ref[i], y).astype(BF16)
        u_ref[0, i * n2:(i + 1) * n2, :] = u[:n2]
        u_ref[1, i * n2:(i + 1) * n2, :] = u[n2:]
    ur = u_ref[0]
    ui = u_ref[1]
    cs = cs_ref[...]
    parts = []
    for g in range(FNET_GROUPS):
        cols = slice(g * FNET_GROUP_DIM, (g + 1) * FNET_GROUP_DIM)
        parts.append(_dot(jnp.concatenate([ur[:, cols], ui[:, cols]], axis=1), cs))
    f = (jnp.concatenate(parts, axis=1) * inv_norm).astype(BF16)
    f = _dot(perm_ref[...], f).astype(BF16)
    r = mod_ref[...][5:6] * _dot(f, w_ref[...])
    out_ref[...] = x_ref[...] + r.reshape(n2, kb, D_MODEL)


def _dft2_residual(y, x, mod, t, cs, perm, w_o, layer, a):
    b, _, n1, n2, _ = y.shape
    kb = DFT_KB
    inv_norm = 1.0 / math.sqrt(n1 * n2 * FNET_GROUP_DIM)
    return pl.pallas_call(
        functools.partial(_dft2_kernel, kb=kb, n2=n2, inv_norm=inv_norm),
        grid=(b, n1 // kb),
        in_specs=[
            pl.BlockSpec((kb, 2 * n2, 2 * n2), lambda i, t: (t, 0, 0)),
            pl.BlockSpec((None, 2, kb, n2, D_MODEL), lambda i, t: (i, 0, t, 0, 0)),
            pl.BlockSpec((None, n2, kb, D_MODEL), lambda i, t: (i, 0, t, 0)),
            pl.BlockSpec((None, None, N_SUBLAYERS * N_MOD, D_MODEL), lambda i, t: (layer, i, 0, 0)),
            pl.BlockSpec((2 * FNET_GROUP_DIM, FNET_GROUP_DIM), lambda i, t: (0, 0)),
            pl.BlockSpec((kb * n2, kb * n2), lambda i, t: (0, 0)),
            pl.BlockSpec((None, D_MODEL, D_MODEL), lambda i, t: (a, 0, 0)),
        ],
        out_specs=pl.BlockSpec((None, n2, kb, D_MODEL), lambda i, t: (i, 0, t, 0)),
        out_shape=jax.ShapeDtypeStruct(x.shape, F32),
        scratch_shapes=[pltpu.VMEM((2, kb * n2, D_MODEL), BF16)],
        compiler_params=_params("parallel", "parallel"),
        name=f"fnet_dft2_l{layer}",
    )(t, y, x, mod, cs, perm, w_o)


def _fourier_mix(x, mod, gain, w_o, tables, layer, a):
    b, s, _ = x.shape
    m1, t, cs, perm = tables
    n1, n2 = _dft_split(s)
    y = _dft1(x.reshape(b, n1, n2, D_MODEL), mod, gain, m1, layer)
    out = _dft2_residual(y, x.reshape(b, n2, n1, D_MODEL), mod, t, cs, perm, w_o, layer, a)
    return out.reshape(b, s, D_MODEL)


def _run_trunk(x, mod, w):
    s = x.shape[1]
    cos, sin = _rope_tables(s)
    tables = _dft_tables(s)
    for l in range(DEPTH):
        gains = w["norm_gain"][l]
        x = _ffn(x, mod, gains[0:1], w["ffn_w_in"], w["ffn_w_out"], l, 0, 0)
        a = l // 2
        if l % 2 == 0:
            q, k4, vt = _qkv(x, mod, gains[1:2], w["attn_w_qkv"], w["q_gain"][a], w["k_gain"][a],
                             cos, sin, w["ones_bd"], l, a)
            o = _attention(q, k4, vt, w["score_bound"][a], l)
            x = _proj_residual(o, x, mod, w["attn_w_o"], l, a)
        else:
            x = _fourier_mix(x, mod, gains[1:2], w["fnet_w_o"], tables, l, a)
        x = _ffn(x, mod, gains[2:3], w["ffn_w_in"], w["ffn_w_out"], l, 1, 2)
    return x


def kernel(x_prompt, x_sample, c_prompt, c_sample, norm_gain, ada_w, ada_b, ffn_w_in, ffn_w_out,
           attn_w_qkv, attn_q_gain, attn_k_gain, attn_w_o, fnet_w_o):
    head = np.arange(Q_DIM) // HEAD_DIM
    w = {
        "norm_gain": norm_gain,
        "ffn_w_in": ffn_w_in.astype(BF16),
        "ffn_w_out": ffn_w_out.astype(BF16),
        "attn_w_qkv": attn_w_qkv.astype(BF16),
        "attn_w_o": attn_w_o.astype(BF16),
        "fnet_w_o": fnet_w_o.astype(BF16),
        "q_gain": jnp.tile(attn_q_gain, (1, N_HEADS))[:, None, :],
        "k_gain": jnp.tile(attn_k_gain, (1, N_KV_HEADS))[:, None, :],
        "ones_bd": jnp.asarray(head[:, None] == head[None, :], BF16),
        "score_bound": (SCORE_BOUND_SLACK * math.sqrt(HEAD_DIM) * LOG2E
                        * jnp.max(jnp.abs(attn_q_gain), axis=1) * jnp.max(jnp.abs(attn_k_gain), axis=1)),
    }
    nb = x_prompt.shape[0]
    mod = _modulation(jnp.concatenate([c_prompt, c_sample], axis=0), ada_w, ada_b)
    y_prompt = _run_trunk(x_prompt, mod[:, :nb], w)
    y_sample = _run_trunk(x_sample, mod[:, nb:], w)
    return (y_prompt, y_sample)
```

```python
import functools
import math

import numpy as np
import jax
import jax.numpy as jnp
from jax import lax
from jax.experimental import pallas as pl
from jax.experimental.pallas import tpu as pltpu

F32 = jnp.float32
BF16 = jnp.bfloat16

D_MODEL = 1024
DEPTH = 4
N_HEADS = 16
N_KV_HEADS = 4
KV_GROUP = N_HEADS // N_KV_HEADS
HEAD_DIM = D_MODEL // N_HEADS
Q_DIM = N_HEADS * HEAD_DIM
KV_DIM = N_KV_HEADS * HEAD_DIM
QKV_DIM = Q_DIM + 2 * KV_DIM
GROUP_W = KV_GROUP * HEAD_DIM
ROPE_AXIS_DIM = HEAD_DIM // 2
ROPE_THETA = 10000.0
FNET_GROUPS = 8
FNET_GROUP_DIM = D_MODEL // FNET_GROUPS
D_FF = 2816
GRID_W = 64
N_SUBLAYERS = 3
N_MOD = 3
EPS = 1e-6
LOG2E = 1.4426950408889634

LANES = 128
VMEM_LIMIT = 48 * 1024 * 1024

FFN_TM = 512
TOK_TM = 512
ATTN_TQ = 256
ATTN_SUB = 2
ATTN_TK = 2048
ATTN_NOSHIFT_MAX_SCORE = 48.0
SCORE_BOUND_SLACK = 1.02
VT_ROWS = HEAD_DIM + 16
DFT_N2_MIN = 64
DFT_N1_MAX = 64
DFT_KB = 8
DFT_SB = 16


def _params(*sem):
    return pltpu.CompilerParams(dimension_semantics=sem, vmem_limit_bytes=VMEM_LIMIT)


def _dot(a, b):
    return jnp.dot(a, b, preferred_element_type=F32)


def _modnorm(x, gain, shift, scale):
    ms = jnp.mean(x * x, axis=-1, keepdims=True)
    return (x * lax.rsqrt(ms + EPS) * gain) * (1.0 + scale) + shift


def _mod_kernel(c_ref, w_ref, b_ref, o_ref):
    c = c_ref[...]
    a = (c * jax.nn.sigmoid(c)).astype(BF16)
    o_ref[...] = _dot(a, w_ref[...].astype(BF16)) + b_ref[...]


def _modulation(c, ada_w, ada_b):
    b = c.shape[0]
    n = N_SUBLAYERS * N_MOD
    out = pl.pallas_call(
        _mod_kernel,
        grid=(DEPTH, n),
        in_specs=[
            pl.BlockSpec((b, D_MODEL), lambda l, j: (0, 0)),
            pl.BlockSpec((None, D_MODEL, D_MODEL), lambda l, j: (l, 0, j)),
            pl.BlockSpec((None, 1, D_MODEL), lambda l, j: (l, 0, j)),
        ],
        out_specs=pl.BlockSpec((None, b, D_MODEL), lambda l, j: (l, 0, j)),
        out_shape=jax.ShapeDtypeStruct((DEPTH, b, n * D_MODEL), F32),
        compiler_params=_params("parallel", "parallel"),
        name="adaln_mod",
    )(c, ada_w, ada_b.reshape(DEPTH, 1, n * D_MODEL))
    return out.reshape(DEPTH, b, n, D_MODEL)


def _ffn_update(x, mod, gain, wg_ref, wu_ref, wo_ref, sub):
    h = _modnorm(x, gain, mod[3 * sub:3 * sub + 1], mod[3 * sub + 1:3 * sub + 2]).astype(BF16)
    g = _dot(h, wg_ref[...])
    u = _dot(h, wu_ref[...])
    a = (g * jax.nn.sigmoid(g) * u).astype(BF16)
    return x + (0.5 * mod[3 * sub + 2:3 * sub + 3]) * _dot(a, wo_ref[...])


def _ffn_kernel(x_ref, mod_ref, gain_ref, wg_ref, wu_ref, wo_ref, o_ref, *, sub):
    o_ref[...] = _ffn_update(x_ref[...], mod_ref[...], gain_ref[...], wg_ref, wu_ref, wo_ref, sub)


def _proj_ffn_kernel(x_ref, a_ref, wp_ref, mod_ref, gain_ref, wg_ref, wu_ref, wo_ref, o_ref, *, sub):
    mod = mod_ref[...]
    x = x_ref[...] + mod[3 * (sub - 1) + 2:3 * (sub - 1) + 3] * _dot(a_ref[...], wp_ref[...])
    o_ref[...] = _ffn_update(x, mod, gain_ref[...], wg_ref, wu_ref, wo_ref, sub)


def _resident(block_shape, index_map):
    return pl.BlockSpec(block_shape, index_map, pipeline_mode=pl.Buffered(1))


def _ffn(x, mod, gain, w_in, w_out, layer, which, sub, mixed=None, w_proj=None, proj_layer=None):
    b, s, _ = x.shape
    tm = min(FFN_TM, s)
    tok = pl.BlockSpec((None, tm, D_MODEL), lambda i, t: (i, t, 0))
    specs = [
        pl.BlockSpec((None, None, N_SUBLAYERS * N_MOD, D_MODEL), lambda i, t: (layer, i, 0, 0)),
        pl.BlockSpec((1, D_MODEL), lambda i, t: (0, 0)),
        _resident((None, None, D_MODEL, D_FF), lambda i, t: (layer, which, 0, 0)),
        _resident((None, None, D_MODEL, D_FF), lambda i, t: (layer, which, 0, 1)),
        _resident((None, None, D_FF, D_MODEL), lambda i, t: (layer, which, 0, 0)),
    ]
    args = (mod, gain, w_in, w_in, w_out)
    if mixed is None:
        body, specs, args = _ffn_kernel, [tok] + specs, (x,) + args
    else:
        proj = _resident((None, D_MODEL, D_MODEL), lambda i, t: (proj_layer, 0, 0))
        body, specs, args = _proj_ffn_kernel, [tok, tok, proj] + specs, (x, mixed, w_proj) + args
    return pl.pallas_call(
        functools.partial(body, sub=sub),
        grid=(b, s // tm),
        in_specs=specs,
        out_specs=tok,
        out_shape=jax.ShapeDtypeStruct(x.shape, F32),
        compiler_params=_params("parallel", "parallel"),
        name=f"ffn_l{layer}_{which}",
    )(*args)


def _head_rms(t, ones_bd, gain):
    t2 = t * t
    hi = t2.astype(BF16)
    lo = (t2 - hi.astype(F32)).astype(BF16)
    ss = _dot(hi, ones_bd) + _dot(lo, ones_bd)
    return t * lax.rsqrt(ss * (1.0 / HEAD_DIM) + EPS) * gain


def _rope(t, cos, sin_signed):
    n = t.shape[1]
    lane = lax.broadcasted_iota(jnp.int32, t.shape, 1)
    nxt = pltpu.roll(t, n - 1, 1)
    prv = pltpu.roll(t, 1, 1)
    partner = jnp.where(lane % 2 == 0, nxt, prv)
    reps = n // LANES
    cosf = jnp.concatenate([cos] * reps, axis=1)
    sinf = jnp.concatenate([sin_signed] * reps, axis=1)
    return t * cosf + partner * sinf


def _repeat_heads(t):
    lane = lax.broadcasted_iota(jnp.int32, (t.shape[0], LANES), 1)
    low = lane < HEAD_DIM
    cols = []
    for j in range(KV_DIM // LANES):
        c = t[:, j * LANES:(j + 1) * LANES]
        sw = pltpu.roll(c, HEAD_DIM, 1)
        first = jnp.where(low, c, sw)
        second = jnp.where(low, sw, c)
        cols += [first] * (GROUP_W // LANES) + [second] * (GROUP_W // LANES)
    return jnp.concatenate(cols, axis=1)


def _qkv_kernel(x_ref, mod_ref, gain_ref, w_ref, qg_ref, kg_ref, cos_ref, sin_ref, bd_ref,
                q_ref, k_ref, vt_ref):
    mod = mod_ref[...]
    h = _modnorm(x_ref[...], gain_ref[...], mod[3:4], mod[4:5]).astype(BF16)
    qkv = _dot(h, w_ref[...])
    q = qkv[:, :Q_DIM]
    k = qkv[:, Q_DIM:Q_DIM + KV_DIM]
    v = qkv[:, Q_DIM + KV_DIM:]
    cos = cos_ref[...]
    sin = sin_ref[...]
    q = _rope(_head_rms(q, bd_ref[...], qg_ref[...]), cos, sin)
    k = _rope(_head_rms(k, bd_ref[:KV_DIM, :KV_DIM], kg_ref[...]), cos, sin)
    q_ref[...] = (q * (LOG2E / math.sqrt(HEAD_DIM))).astype(BF16)
    k_ref[...] = _repeat_heads(k).astype(BF16)
    v_t = v.T
    ones = jnp.ones((VT_ROWS - HEAD_DIM, v.shape[0]), BF16)
    for j in range(N_KV_HEADS):
        vt_ref[j, :HEAD_DIM, :] = v_t[j * HEAD_DIM:(j + 1) * HEAD_DIM].astype(BF16)
        vt_ref[j, HEAD_DIM:, :] = ones


def _rope_tables(s):
    rows = s // GRID_W
    r = jnp.repeat(jnp.arange(rows), GRID_W).astype(F32)
    c = jnp.tile(jnp.arange(GRID_W), rows).astype(F32)
    inv = ROPE_THETA ** (-jnp.arange(0, ROPE_AXIS_DIM, 2, dtype=F32) / ROPE_AXIS_DIM)
    ang = jnp.concatenate([r[:, None] * inv, c[:, None] * inv], axis=-1)
    cos = jnp.repeat(jnp.cos(ang), 2, axis=-1)
    sin = jnp.repeat(jnp.sin(ang), 2, axis=-1)
    sign = jnp.tile(jnp.array([-1.0, 1.0], F32), HEAD_DIM // 2)
    reps = LANES // HEAD_DIM
    return jnp.tile(cos, (1, reps)), jnp.tile(sin * sign, (1, reps))


def _qkv(x, mod, gain, w_qkv, q_gain, k_gain, cos, sin, ones_bd, layer, a):
    b, s, _ = x.shape
    tm = min(TOK_TM, s)
    tok = lambda i, t: (i, t, 0)
    const2 = lambda i, t: (0, 0)
    shape = jax.ShapeDtypeStruct((b, s, D_MODEL), BF16)
    tk = min(ATTN_TK, s)
    per = tk // tm
    vt_shape = jax.ShapeDtypeStruct((b, N_KV_HEADS, s // tk, VT_ROWS, tk), BF16)
    vt_spec = pl.BlockSpec((None, N_KV_HEADS, None, VT_ROWS, tm), lambda i, t: (i, 0, t // per, 0, t % per))
    return pl.pallas_call(
        _qkv_kernel,
        grid=(b, s // tm),
        in_specs=[
            pl.BlockSpec((None, tm, D_MODEL), tok),
            pl.BlockSpec((None, None, N_SUBLAYERS * N_MOD, D_MODEL), lambda i, t: (layer, i, 0, 0)),
            pl.BlockSpec((1, D_MODEL), const2),
            pl.BlockSpec((None, D_MODEL, QKV_DIM), lambda i, t: (a, 0, 0)),
            pl.BlockSpec((1, Q_DIM), const2),
            pl.BlockSpec((1, KV_DIM), const2),
            pl.BlockSpec((tm, LANES), lambda i, t: (t, 0)),
            pl.BlockSpec((tm, LANES), lambda i, t: (t, 0)),
            pl.BlockSpec((Q_DIM, Q_DIM), const2),
        ],
        out_specs=[pl.BlockSpec((None, tm, D_MODEL), tok)] * 2 + [vt_spec],
        out_shape=[shape, shape, vt_shape],
        compiler_params=_params("parallel", "parallel"),
        name=f"qkv_l{layer}",
    )(x, mod, gain, w_qkv, q_gain, k_gain, cos, sin, ones_bd)


def _attn_kernel(q_ref, k_ref, vt_ref, o_ref, *, tq, nsub, tk, nk, shift):
    head = lax.broadcasted_iota(jnp.int32, (tq, GROUP_W), 1) // HEAD_DIM
    w = KV_GROUP * tq
    qs = []
    for u in range(nsub):
        q = q_ref[u * tq:(u + 1) * tq, :]
        zero = jnp.zeros_like(q)
        qs.append(jnp.concatenate([jnp.where(head == g, q, zero) for g in range(KV_GROUP)], axis=0))

    def scores(c, u):
        kc = k_ref[pl.ds(pl.multiple_of(c * tk, tk), tk), :]
        return lax.dot_general(kc, qs[u], (((1,), (1,)), ((), ())), preferred_element_type=F32)

    def chunk_shifted(c, carry):
        out = []
        for u, (m, acc) in enumerate(carry):
            st = scores(c, u)
            m_new = jnp.maximum(m, jnp.max(st, axis=0, keepdims=True))
            p = jnp.exp2(st - m_new).astype(BF16)
            out.append((m_new, jnp.exp2(m - m_new) * acc + _dot(vt_ref[c], p)))
        return tuple(out)

    def chunk_plain(c, accs):
        return tuple(acc + _dot(vt_ref[c], jnp.exp2(scores(c, u)).astype(BF16)) for u, acc in enumerate(accs))

    acc0 = jnp.zeros((VT_ROWS, w), F32)
    if shift:
        carry = lax.fori_loop(0, nk, chunk_shifted, ((jnp.full((1, w), -1e30, F32), acc0),) * nsub)
        accs = [acc for _, acc in carry]
    else:
        accs = lax.fori_loop(0, nk, chunk_plain, (acc0,) * nsub)
    for u, acc in enumerate(accs):
        out = acc[:HEAD_DIM] / acc[HEAD_DIM:HEAD_DIM + 1]
        o_t = jnp.concatenate([out[:, g * tq:(g + 1) * tq] for g in range(KV_GROUP)], axis=0)
        o_ref[u * tq:(u + 1) * tq, :] = o_t.T.astype(BF16)


def _attention(q, k4, vt, score_bound, layer):
    b, s, _ = q.shape
    nk, tk = vt.shape[2], vt.shape[4]
    tq = min(ATTN_TQ, s)
    nsub = min(ATTN_SUB, s // tq)
    rows = nsub * tq

    def call(shift):
        return pl.pallas_call(
            functools.partial(_attn_kernel, tq=tq, nsub=nsub, tk=tk, nk=nk, shift=shift),
            grid=(b, N_KV_HEADS, s // rows),
            in_specs=[
                pl.BlockSpec((None, rows, GROUP_W), lambda i, j, t: (i, t, j)),
                pl.BlockSpec((None, s, GROUP_W), lambda i, j, t: (i, 0, j)),
                pl.BlockSpec((None, None, nk, VT_ROWS, tk), lambda i, j, t: (i, j, 0, 0, 0)),
            ],
            out_specs=pl.BlockSpec((None, rows, GROUP_W), lambda i, j, t: (i, t, j)),
            out_shape=jax.ShapeDtypeStruct((b, s, D_MODEL), BF16),
            compiler_params=_params("parallel", "parallel", "parallel"),
            name=f"attn_l{layer}_{'shifted' if shift else 'plain'}",
        )

    return lax.cond(score_bound <= ATTN_NOSHIFT_MAX_SCORE,
                    lambda *a: call(False)(*a), lambda *a: call(True)(*a), q, k4, vt)


def _dft_split(s):
    n2 = max(DFT_N2_MIN, s // DFT_N1_MAX)
    return s // n2, n2


def _dft_tables(s):
    n1, n2 = _dft_split(s)
    k1 = np.arange(n1)
    ang1 = 2.0 * np.pi * ((k1[:, None] * k1[None, :]) % n1) / n1
    m1 = np.concatenate([np.cos(ang1), -np.sin(ang1)], axis=0)
    m1 = np.kron(m1, np.eye(DFT_SB))
    src = (np.arange(DFT_KB)[None, :] * n2 + np.arange(n2)[:, None]).reshape(-1)
    perm = np.zeros((DFT_KB * n2, DFT_KB * n2))
    perm[np.arange(DFT_KB * n2), src] = 1.0
    s2 = np.arange(n2)
    num = (s2[None, None, :] * k1[:, None, None] + s2[None, None, :] * s2[None, :, None] * n1) % s
    psi = 2.0 * np.pi * num / s
    tr, ti = np.cos(psi), -np.sin(psi)
    t = np.concatenate([np.concatenate([tr, -ti], axis=2), np.concatenate([ti, tr], axis=2)], axis=1)
    c = np.arange(FNET_GROUP_DIM)
    phi = 2.0 * np.pi * ((c[:, None] * c[None, :]) % FNET_GROUP_DIM) / FNET_GROUP_DIM
    cs = np.concatenate([np.cos(phi), np.sin(phi)], axis=0)
    return jnp.asarray(m1, BF16), jnp.asarray(t, BF16), jnp.asarray(cs, BF16), jnp.asarray(perm, BF16)


def _dft1_kernel(x_ref, mod_ref, gain_ref, m_ref, y_ref, *, n1, sb):
    mod = mod_ref[...]
    x = x_ref[...].reshape(n1 * sb, D_MODEL)
    h = _modnorm(x, gain_ref[...], mod[3:4], mod[4:5]).astype(BF16)
    y_ref[...] = _dot(m_ref[...], h).astype(BF16).reshape(2, n1, sb, D_MODEL)


def _dft1(x, mod, gain, m1, layer):
    b, n1, n2, _ = x.shape
    sb = DFT_SB
    return pl.pallas_call(
        functools.partial(_dft1_kernel, n1=n1, sb=sb),
        grid=(b, n2 // sb),
        in_specs=[
            pl.BlockSpec((None, n1, sb, D_MODEL), lambda i, t: (i, 0, t, 0)),
            pl.BlockSpec((None, None, N_SUBLAYERS * N_MOD, D_MODEL), lambda i, t: (layer, i, 0, 0)),
            pl.BlockSpec((1, D_MODEL), lambda i, t: (0, 0)),
            _resident((2 * n1 * sb, n1 * sb), lambda i, t: (0, 0)),
        ],
        out_specs=pl.BlockSpec((None, 2, n1, sb, D_MODEL), lambda i, t: (i, 0, 0, t, 0)),
        out_shape=jax.ShapeDtypeStruct((b, 2, n1, n2, D_MODEL), BF16),
        compiler_params=_params("parallel", "parallel"),
        name=f"fnet_dft1_l{layer}",
    )(x, mod, gain, m1)


def _dft2_kernel(t_ref, y_ref, x_ref, mod_ref, cs_ref, perm_ref, w_ref, out_ref, u_ref, *, kb, n2, inv_norm):
    for i in range(kb):
        y = jnp.concatenate([y_ref[0, i], y_ref[1, i]], axis=0)
        u = _dot(t_ref[i], y).astype(BF16)
        u_ref[0, i * n2:(i + 1) * n2, :] = u[:n2]
        u_ref[1, i * n2:(i + 1) * n2, :] = u[n2:]
    ur = u_ref[0]
    ui = u_ref[1]
    cs = cs_ref[...]
    parts = []
    for g in range(FNET_GROUPS):
        cols = slice(g * FNET_GROUP_DIM, (g + 1) * FNET_GROUP_DIM)
        parts.append(_dot(jnp.concatenate([ur[:, cols], ui[:, cols]], axis=1), cs))
    f = (jnp.concatenate(parts, axis=1) * inv_norm).astype(BF16)
    f = _dot(perm_ref[...], f).astype(BF16)
    r = mod_ref[...][5:6] * _dot(f, w_ref[...])
    out_ref[...] = x_ref[...] + r.reshape(n2, kb, D_MODEL)


def _dft2_residual(y, x, mod, t, cs, perm, w_o, layer, a):
    b, _, n1, n2, _ = y.shape
    kb = DFT_KB
    inv_norm = 1.0 / math.sqrt(n1 * n2 * FNET_GROUP_DIM)
    return pl.pallas_call(
        functools.partial(_dft2_kernel, kb=kb, n2=n2, inv_norm=inv_norm),
        grid=(b, n1 // kb),
        in_specs=[
            pl.BlockSpec((kb, 2 * n2, 2 * n2), lambda i, t: (t, 0, 0)),
            pl.BlockSpec((None, 2, kb, n2, D_MODEL), lambda i, t: (i, 0, t, 0, 0)),
            pl.BlockSpec((None, n2, kb, D_MODEL), lambda i, t: (i, 0, t, 0)),
            pl.BlockSpec((None, None, N_SUBLAYERS * N_MOD, D_MODEL), lambda i, t: (layer, i, 0, 0)),
            pl.BlockSpec((2 * FNET_GROUP_DIM, FNET_GROUP_DIM), lambda i, t: (0, 0)),
            pl.BlockSpec((kb * n2, kb * n2), lambda i, t: (0, 0)),
            pl.BlockSpec((None, D_MODEL, D_MODEL), lambda i, t: (a, 0, 0)),
        ],
        out_specs=pl.BlockSpec((None, n2, kb, D_MODEL), lambda i, t: (i, 0, t, 0)),
        out_shape=jax.ShapeDtypeStruct(x.shape, F32),
        scratch_shapes=[pltpu.VMEM((2, kb * n2, D_MODEL), BF16)],
        compiler_params=_params("parallel", "parallel"),
        name=f"fnet_dft2_l{layer}",
    )(t, y, x, mod, cs, perm, w_o)


def _fourier_mix(x, mod, gain, w_o, tables, layer, a):
    b, s, _ = x.shape
    m1, t, cs, perm = tables
    n1, n2 = _dft_split(s)
    y = _dft1(x.reshape(b, n1, n2, D_MODEL), mod, gain, m1, layer)
    out = _dft2_residual(y, x.reshape(b, n2, n1, D_MODEL), mod, t, cs, perm, w_o, layer, a)
    return out.reshape(b, s, D_MODEL)


def _run_trunk(x, mod, w):
    s = x.shape[1]
    cos, sin = _rope_tables(s)
    tables = _dft_tables(s)
    for l in range(DEPTH):
        gains = w["norm_gain"][l]
        x = _ffn(x, mod, gains[0:1], w["ffn_w_in"], w["ffn_w_out"], l, 0, 0)
        a = l // 2
        if l % 2 == 0:
            q, k4, vt = _qkv(x, mod, gains[1:2], w["attn_w_qkv"], w["q_gain"][a], w["k_gain"][a],
                             cos, sin, w["ones_bd"], l, a)
            o = _attention(q, k4, vt, w["score_bound"][a], l)
            x = _ffn(x, mod, gains[2:3], w["ffn_w_in"], w["ffn_w_out"], l, 1, 2,
                     mixed=o, w_proj=w["attn_w_o"], proj_layer=a)
        else:
            x = _fourier_mix(x, mod, gains[1:2], w["fnet_w_o"], tables, l, a)
            x = _ffn(x, mod, gains[2:3], w["ffn_w_in"], w["ffn_w_out"], l, 1, 2)
    return x


def kernel(x_prompt, x_sample, c_prompt, c_sample, norm_gain, ada_w, ada_b, ffn_w_in, ffn_w_out,
           attn_w_qkv, attn_q_gain, attn_k_gain, attn_w_o, fnet_w_o):
    head = np.arange(Q_DIM) // HEAD_DIM
    w = {
        "norm_gain": norm_gain,
        "ffn_w_in": ffn_w_in.astype(BF16),
        "ffn_w_out": ffn_w_out.astype(BF16),
        "attn_w_qkv": attn_w_qkv.astype(BF16),
        "attn_w_o": attn_w_o.astype(BF16),
        "fnet_w_o": fnet_w_o.astype(BF16),
        "q_gain": jnp.tile(attn_q_gain, (1, N_HEADS))[:, None, :],
        "k_gain": jnp.tile(attn_k_gain, (1, N_KV_HEADS))[:, None, :],
        "ones_bd": jnp.asarray(head[:, None] == head[None, :], BF16),
        "score_bound": (SCORE_BOUND_SLACK * math.sqrt(HEAD_DIM) * LOG2E
                        * jnp.max(jnp.abs(attn_q_gain), axis=1) * jnp.max(jnp.abs(attn_k_gain), axis=1)),
    }
    nb = x_prompt.shape[0]
    mod = _modulation(jnp.concatenate([c_prompt, c_sample], axis=0), ada_w, ada_b)
    y_prompt = _run_trunk(x_prompt, mod[:, :nb], w)
    y_sample = _run_trunk(x_sample, mod[:, nb:], w)
    return (y_prompt, y_sample)
```

```python
import functools
import math

import numpy as np
import jax
import jax.numpy as jnp
from jax import lax
from jax.experimental import pallas as pl
from jax.experimental.pallas import tpu as pltpu

F32 = jnp.float32
BF16 = jnp.bfloat16

D_MODEL = 1024
DEPTH = 4
N_HEADS = 16
N_KV_HEADS = 4
KV_GROUP = N_HEADS // N_KV_HEADS
HEAD_DIM = D_MODEL // N_HEADS
Q_DIM = N_HEADS * HEAD_DIM
KV_DIM = N_KV_HEADS * HEAD_DIM
QKV_DIM = Q_DIM + 2 * KV_DIM
GROUP_W = KV_GROUP * HEAD_DIM
ROPE_AXIS_DIM = HEAD_DIM // 2
ROPE_THETA = 10000.0
FNET_GROUPS = 8
FNET_GROUP_DIM = D_MODEL // FNET_GROUPS
D_FF = 2816
GRID_W = 64
N_SUBLAYERS = 3
N_MOD = 3
EPS = 1e-6
LOG2E = 1.4426950408889634

LANES = 128
VMEM_LIMIT = 48 * 1024 * 1024

FFN_TM = 512
TOK_TM = 512
QKV_SUB = 2
ATTN_TQ = 256
ATTN_SUB = 2
ATTN_TK = 2048
ATTN_NOSHIFT_MAX_SCORE = 48.0
SCORE_BOUND_SLACK = 1.02
VT_ROWS = HEAD_DIM + 16
DFT_N2_MIN = 64
DFT_N1_MAX = 64
DFT_KB = 8
DFT_SB = 16


def _params(*sem):
    return pltpu.CompilerParams(dimension_semantics=sem, vmem_limit_bytes=VMEM_LIMIT)


def _dot(a, b):
    return jnp.dot(a, b, preferred_element_type=F32)


def _modnorm(x, gain, shift, scale):
    ms = jnp.mean(x * x, axis=-1, keepdims=True)
    return (x * lax.rsqrt(ms + EPS) * gain) * (1.0 + scale) + shift


def _mod_kernel(c_ref, w_ref, b_ref, o_ref):
    c = c_ref[...]
    a = (c * jax.nn.sigmoid(c)).astype(BF16)
    o_ref[...] = _dot(a, w_ref[...].astype(BF16)) + b_ref[...]


def _modulation(c, ada_w, ada_b):
    b = c.shape[0]
    n = N_SUBLAYERS * N_MOD
    out = pl.pallas_call(
        _mod_kernel,
        grid=(DEPTH, n),
        in_specs=[
            pl.BlockSpec((b, D_MODEL), lambda l, j: (0, 0)),
            pl.BlockSpec((None, D_MODEL, D_MODEL), lambda l, j: (l, 0, j)),
            pl.BlockSpec((None, 1, D_MODEL), lambda l, j: (l, 0, j)),
        ],
        out_specs=pl.BlockSpec((None, b, D_MODEL), lambda l, j: (l, 0, j)),
        out_shape=jax.ShapeDtypeStruct((DEPTH, b, n * D_MODEL), F32),
        compiler_params=_params("parallel", "parallel"),
        name="adaln_mod",
    )(c, ada_w, ada_b.reshape(DEPTH, 1, n * D_MODEL))
    return out.reshape(DEPTH, b, n, D_MODEL)


def _ffn_update(x, mod, gain, wg_ref, wu_ref, wo_ref, sub):
    h = _modnorm(x, gain, mod[3 * sub:3 * sub + 1], mod[3 * sub + 1:3 * sub + 2]).astype(BF16)
    g = _dot(h, wg_ref[...])
    u = _dot(h, wu_ref[...])
    a = (g * jax.nn.sigmoid(g) * u).astype(BF16)
    return x + (0.5 * mod[3 * sub + 2:3 * sub + 3]) * _dot(a, wo_ref[...])


def _ffn_kernel(x_ref, mod_ref, gain_ref, wg_ref, wu_ref, wo_ref, o_ref, *, sub):
    o_ref[...] = _ffn_update(x_ref[...], mod_ref[...], gain_ref[...], wg_ref, wu_ref, wo_ref, sub)


def _proj_ffn_kernel(x_ref, a_ref, wp_ref, mod_ref, gain_ref, wg_ref, wu_ref, wo_ref, o_ref, *, sub):
    mod = mod_ref[...]
    x = x_ref[...] + mod[3 * (sub - 1) + 2:3 * (sub - 1) + 3] * _dot(a_ref[...], wp_ref[...])
    o_ref[...] = _ffn_update(x, mod, gain_ref[...], wg_ref, wu_ref, wo_ref, sub)


def _resident(block_shape, index_map):
    return pl.BlockSpec(block_shape, index_map, pipeline_mode=pl.Buffered(1))


def _ffn(x, mod, gain, w_in, w_out, layer, which, sub, mixed=None, w_proj=None, proj_layer=None):
    b, s, _ = x.shape
    tm = min(FFN_TM, s)
    tok = pl.BlockSpec((None, tm, D_MODEL), lambda i, t: (i, t, 0))
    specs = [
        pl.BlockSpec((None, None, N_SUBLAYERS * N_MOD, D_MODEL), lambda i, t: (layer, i, 0, 0)),
        pl.BlockSpec((1, D_MODEL), lambda i, t: (0, 0)),
        _resident((None, None, D_MODEL, D_FF), lambda i, t: (layer, which, 0, 0)),
        _resident((None, None, D_MODEL, D_FF), lambda i, t: (layer, which, 0, 1)),
        _resident((None, None, D_FF, D_MODEL), lambda i, t: (layer, which, 0, 0)),
    ]
    args = (mod, gain, w_in, w_in, w_out)
    if mixed is None:
        body, specs, args = _ffn_kernel, [tok] + specs, (x,) + args
    else:
        proj = _resident((None, D_MODEL, D_MODEL), lambda i, t: (proj_layer, 0, 0))
        body, specs, args = _proj_ffn_kernel, [tok, tok, proj] + specs, (x, mixed, w_proj) + args
    return pl.pallas_call(
        functools.partial(body, sub=sub),
        grid=(b, s // tm),
        in_specs=specs,
        out_specs=tok,
        out_shape=jax.ShapeDtypeStruct(x.shape, F32),
        compiler_params=_params("parallel", "parallel"),
        name=f"ffn_l{layer}_{which}",
    )(*args)


def _head_rms(t, ones_bd, gain):
    ss = _dot((t * t).astype(BF16), ones_bd)
    return t * lax.rsqrt(ss * (1.0 / HEAD_DIM) + EPS) * gain


def _rope(t, cos, sin_signed):
    n = t.shape[1]
    lane = lax.broadcasted_iota(jnp.int32, t.shape, 1)
    nxt = pltpu.roll(t, n - 1, 1)
    prv = pltpu.roll(t, 1, 1)
    partner = jnp.where(lane % 2 == 0, nxt, prv)
    reps = n // LANES
    cosf = jnp.concatenate([cos] * reps, axis=1)
    sinf = jnp.concatenate([sin_signed] * reps, axis=1)
    return t * cosf + partner * sinf


def _repeat_heads(t):
    lane = lax.broadcasted_iota(jnp.int32, (t.shape[0], LANES), 1)
    low = lane < HEAD_DIM
    cols = []
    for j in range(KV_DIM // LANES):
        c = t[:, j * LANES:(j + 1) * LANES]
        sw = pltpu.roll(c, HEAD_DIM, 1)
        first = jnp.where(low, c, sw)
        second = jnp.where(low, sw, c)
        cols += [first] * (GROUP_W // LANES) + [second] * (GROUP_W // LANES)
    return jnp.concatenate(cols, axis=1)


def _qkv_kernel(x_ref, mod_ref, gain_ref, w_ref, qg_ref, kg_ref, cos_ref, sin_ref, bd_ref,
                q_ref, k_ref, vt_ref):
    mod = mod_ref[...]
    tm = x_ref.shape[0]
    half = tm // QKV_SUB
    for u in range(QKV_SUB):
        rows = slice(u * half, (u + 1) * half)
        h = _modnorm(x_ref[rows, :], gain_ref[...], mod[3:4], mod[4:5]).astype(BF16)
        qkv = _dot(h, w_ref[...])
        q = qkv[:, :Q_DIM]
        k = qkv[:, Q_DIM:Q_DIM + KV_DIM]
        v = qkv[:, Q_DIM + KV_DIM:]
        cos = cos_ref[rows, :]
        sin = sin_ref[rows, :]
        q = _rope(_head_rms(q, bd_ref[...], qg_ref[...]), cos, sin)
        k = _rope(_head_rms(k, bd_ref[:KV_DIM, :KV_DIM], kg_ref[...]), cos, sin)
        q_ref[rows, :] = (q * (LOG2E / math.sqrt(HEAD_DIM))).astype(BF16)
        k_ref[rows, :] = _repeat_heads(k).astype(BF16)
        v_t = v.T
        ones = jnp.ones((VT_ROWS - HEAD_DIM, half), BF16)
        for j in range(N_KV_HEADS):
            vt_ref[j, :HEAD_DIM, rows] = v_t[j * HEAD_DIM:(j + 1) * HEAD_DIM].astype(BF16)
            vt_ref[j, HEAD_DIM:, rows] = ones


def _rope_tables(s):
    rows = s // GRID_W
    r = jnp.repeat(jnp.arange(rows), GRID_W).astype(F32)
    c = jnp.tile(jnp.arange(GRID_W), rows).astype(F32)
    inv = ROPE_THETA ** (-jnp.arange(0, ROPE_AXIS_DIM, 2, dtype=F32) / ROPE_AXIS_DIM)
    ang = jnp.concatenate([r[:, None] * inv, c[:, None] * inv], axis=-1)
    cos = jnp.repeat(jnp.cos(ang), 2, axis=-1)
    sin = jnp.repeat(jnp.sin(ang), 2, axis=-1)
    sign = jnp.tile(jnp.array([-1.0, 1.0], F32), HEAD_DIM // 2)
    reps = LANES // HEAD_DIM
    return jnp.tile(cos, (1, reps)), jnp.tile(sin * sign, (1, reps))


def _qkv(x, mod, gain, w_qkv, q_gain, k_gain, cos, sin, ones_bd, layer, a):
    b, s, _ = x.shape
    tm = min(TOK_TM, s)
    tok = lambda i, t: (i, t, 0)
    const2 = lambda i, t: (0, 0)
    shape = jax.ShapeDtypeStruct((b, s, D_MODEL), BF16)
    tk = min(ATTN_TK, s)
    per = tk // tm
    vt_shape = jax.ShapeDtypeStruct((b, N_KV_HEADS, s // tk, VT_ROWS, tk), BF16)
    vt_spec = pl.BlockSpec((None, N_KV_HEADS, None, VT_ROWS, tm), lambda i, t: (i, 0, t // per, 0, t % per))
    return pl.pallas_call(
        _qkv_kernel,
        grid=(b, s // tm),
        in_specs=[
            pl.BlockSpec((None, tm, D_MODEL), tok),
            pl.BlockSpec((None, None, N_SUBLAYERS * N_MOD, D_MODEL), lambda i, t: (layer, i, 0, 0)),
            pl.BlockSpec((1, D_MODEL), const2),
            pl.BlockSpec((None, D_MODEL, QKV_DIM), lambda i, t: (a, 0, 0)),
            pl.BlockSpec((1, Q_DIM), const2),
            pl.BlockSpec((1, KV_DIM), const2),
            pl.BlockSpec((tm, LANES), lambda i, t: (t, 0)),
            pl.BlockSpec((tm, LANES), lambda i, t: (t, 0)),
            pl.BlockSpec((Q_DIM, Q_DIM), const2),
        ],
        out_specs=[pl.BlockSpec((None, tm, D_MODEL), tok)] * 2 + [vt_spec],
        out_shape=[shape, shape, vt_shape],
        compiler_params=_params("parallel", "parallel"),
        name=f"qkv_l{layer}",
    )(x, mod, gain, w_qkv, q_gain, k_gain, cos, sin, ones_bd)


def _attn_kernel(q_ref, k_ref, vt_ref, o_ref, *, tq, nsub, tk, nk, shift):
    head = lax.broadcasted_iota(jnp.int32, (tq, GROUP_W), 1) // HEAD_DIM
    w = KV_GROUP * tq
    qs = []
    for u in range(nsub):
        q = q_ref[u * tq:(u + 1) * tq, :]
        zero = jnp.zeros_like(q)
        qs.append(jnp.concatenate([jnp.where(head == g, q, zero) for g in range(KV_GROUP)], axis=0))

    def scores(c, u):
        kc = k_ref[pl.ds(pl.multiple_of(c * tk, tk), tk), :]
        return lax.dot_general(kc, qs[u], (((1,), (1,)), ((), ())), preferred_element_type=F32)

    def chunk_shifted(c, carry):
        out = []
        for u, (m, acc) in enumerate(carry):
            st = scores(c, u)
            m_new = jnp.maximum(m, jnp.max(st, axis=0, keepdims=True))
            p = jnp.exp2(st - m_new).astype(BF16)
            out.append((m_new, jnp.exp2(m - m_new) * acc + _dot(vt_ref[c], p)))
        return tuple(out)

    def chunk_plain(c, accs):
        return tuple(acc + _dot(vt_ref[c], jnp.exp2(scores(c, u)).astype(BF16)) for u, acc in enumerate(accs))

    acc0 = jnp.zeros((VT_ROWS, w), F32)
    if shift:
        carry = lax.fori_loop(0, nk, chunk_shifted, ((jnp.full((1, w), -1e30, F32), acc0),) * nsub)
        accs = [acc for _, acc in carry]
    else:
        accs = lax.fori_loop(0, nk, chunk_plain, (acc0,) * nsub)
    for u, acc in enumerate(accs):
        out = acc[:HEAD_DIM] / acc[HEAD_DIM:HEAD_DIM + 1]
        o_t = jnp.concatenate([out[:, g * tq:(g + 1) * tq] for g in range(KV_GROUP)], axis=0)
        o_ref[u * tq:(u + 1) * tq, :] = o_t.T.astype(BF16)


def _attention(q, k4, vt, score_bound, layer):
    b, s, _ = q.shape
    nk, tk = vt.shape[2], vt.shape[4]
    tq = min(ATTN_TQ, s)
    nsub = min(ATTN_SUB, s // tq)
    rows = nsub * tq

    def call(shift):
        return pl.pallas_call(
            functools.partial(_attn_kernel, tq=tq, nsub=nsub, tk=tk, nk=nk, shift=shift),
            grid=(b, N_KV_HEADS, s // rows),
            in_specs=[
                pl.BlockSpec((None, rows, GROUP_W), lambda i, j, t: (i, t, j)),
                pl.BlockSpec((None, s, GROUP_W), lambda i, j, t: (i, 0, j)),
                pl.BlockSpec((None, None, nk, VT_ROWS, tk), lambda i, j, t: (i, j, 0, 0, 0)),
            ],
            out_specs=pl.BlockSpec((None, rows, GROUP_W), lambda i, j, t: (i, t, j)),
            out_shape=jax.ShapeDtypeStruct((b, s, D_MODEL), BF16),
            compiler_params=_params("parallel", "parallel", "parallel"),
            name=f"attn_l{layer}_{'shifted' if shift else 'plain'}",
        )

    return lax.cond(score_bound <= ATTN_NOSHIFT_MAX_SCORE,
                    lambda *a: call(False)(*a), lambda *a: call(True)(*a), q, k4, vt)


def _dft_split(s):
    n2 = max(DFT_N2_MIN, s // DFT_N1_MAX)
    return s // n2, n2


def _dft_tables(s):
    n1, n2 = _dft_split(s)
    k1 = np.arange(n1)
    ang1 = 2.0 * np.pi * ((k1[:, None] * k1[None, :]) % n1) / n1
    m1 = np.concatenate([np.cos(ang1), -np.sin(ang1)], axis=0)
    m1 = np.kron(m1, np.eye(DFT_SB))
    src = (np.arange(DFT_KB)[None, :] * n2 + np.arange(n2)[:, None]).reshape(-1)
    perm = np.zeros((DFT_KB * n2, DFT_KB * n2))
    perm[np.arange(DFT_KB * n2), src] = 1.0
    s2 = np.arange(n2)
    num = (s2[None, None, :] * k1[:, None, None] + s2[None, None, :] * s2[None, :, None] * n1) % s
    psi = 2.0 * np.pi * num / s
    tr, ti = np.cos(psi), -np.sin(psi)
    t = np.concatenate([np.concatenate([tr, -ti], axis=2), np.concatenate([ti, tr], axis=2)], axis=1)
    c = np.arange(FNET_GROUP_DIM)
    phi = 2.0 * np.pi * ((c[:, None] * c[None, :]) % FNET_GROUP_DIM) / FNET_GROUP_DIM
    cs = np.concatenate([np.cos(phi), np.sin(phi)], axis=0)
    return jnp.asarray(m1, BF16), jnp.asarray(t, BF16), jnp.asarray(cs, BF16), jnp.asarray(perm, BF16)


def _dft1_kernel(x_ref, mod_ref, gain_ref, m_ref, y_ref, *, n1, sb):
    mod = mod_ref[...]
    x = x_ref[...].reshape(n1 * sb, D_MODEL)
    h = _modnorm(x, gain_ref[...], mod[3:4], mod[4:5]).astype(BF16)
    y_ref[...] = _dot(m_ref[...], h).astype(BF16).reshape(2, n1, sb, D_MODEL)


def _dft1(x, mod, gain, m1, layer):
    b, n1, n2, _ = x.shape
    sb = DFT_SB
    return pl.pallas_call(
        functools.partial(_dft1_kernel, n1=n1, sb=sb),
        grid=(b, n2 // sb),
        in_specs=[
            pl.BlockSpec((None, n1, sb, D_MODEL), lambda i, t: (i, 0, t, 0)),
            pl.BlockSpec((None, None, N_SUBLAYERS * N_MOD, D_MODEL), lambda i, t: (layer, i, 0, 0)),
            pl.BlockSpec((1, D_MODEL), lambda i, t: (0, 0)),
            _resident((2 * n1 * sb, n1 * sb), lambda i, t: (0, 0)),
        ],
        out_specs=pl.BlockSpec((None, 2, n1, sb, D_MODEL), lambda i, t: (i, 0, 0, t, 0)),
        out_shape=jax.ShapeDtypeStruct((b, 2, n1, n2, D_MODEL), BF16),
        compiler_params=_params("parallel", "parallel"),
        name=f"fnet_dft1_l{layer}",
    )(x, mod, gain, m1)


def _dft2_kernel(t_ref, y_ref, x_ref, mod_ref, cs_ref, perm_ref, w_ref, out_ref, u_ref, *, kb, n2, inv_norm):
    for i in range(kb):
        y = jnp.concatenate([y_ref[0, i], y_ref[1, i]], axis=0)
        u = _dot(t_ref[i], y).astype(BF16)
        u_ref[0, i * n2:(i + 1) * n2, :] = u[:n2]
        u_ref[1, i * n2:(i + 1) * n2, :] = u[n2:]
    ur = u_ref[0]
    ui = u_ref[1]
    cs = cs_ref[...]
    parts = []
    for g in range(FNET_GROUPS):
        cols = slice(g * FNET_GROUP_DIM, (g + 1) * FNET_GROUP_DIM)
        parts.append(_dot(jnp.concatenate([ur[:, cols], ui[:, cols]], axis=1), cs))
    f = (jnp.concatenate(parts, axis=1) * inv_norm).astype(BF16)
    f = _dot(perm_ref[...], f).astype(BF16)
    r = mod_ref[...][5:6] * _dot(f, w_ref[...])
    out_ref[...] = x_ref[...] + r.reshape(n2, kb, D_MODEL)


def _dft2_residual(y, x, mod, t, cs, perm, w_o, layer, a):
    b, _, n1, n2, _ = y.shape
    kb = DFT_KB
    inv_norm = 1.0 / math.sqrt(n1 * n2 * FNET_GROUP_DIM)
    return pl.pallas_call(
        functools.partial(_dft2_kernel, kb=kb, n2=n2, inv_norm=inv_norm),
        grid=(b, n1 // kb),
        in_specs=[
            pl.BlockSpec((kb, 2 * n2, 2 * n2), lambda i, t: (t, 0, 0)),
            pl.BlockSpec((None, 2, kb, n2, D_MODEL), lambda i, t: (i, 0, t, 0, 0)),
            pl.BlockSpec((None, n2, kb, D_MODEL), lambda i, t: (i, 0, t, 0)),
            pl.BlockSpec((None, None, N_SUBLAYERS * N_MOD, D_MODEL), lambda i, t: (layer, i, 0, 0)),
            pl.BlockSpec((2 * FNET_GROUP_DIM, FNET_GROUP_DIM), lambda i, t: (0, 0)),
            pl.BlockSpec((kb * n2, kb * n2), lambda i, t: (0, 0)),
            pl.BlockSpec((None, D_MODEL, D_MODEL), lambda i, t: (a, 0, 0)),
        ],
        out_specs=pl.BlockSpec((None, n2, kb, D_MODEL), lambda i, t: (i, 0, t, 0)),
        out_shape=jax.ShapeDtypeStruct(x.shape, F32),
        scratch_shapes=[pltpu.VMEM((2, kb * n2, D_MODEL), BF16)],
        compiler_params=_params("parallel", "parallel"),
        name=f"fnet_dft2_l{layer}",
    )(t, y, x, mod, cs, perm, w_o)


def _fourier_mix(x, mod, gain, w_o, tables, layer, a):
    b, s, _ = x.shape
    m1, t, cs, perm = tables
    n1, n2 = _dft_split(s)
    y = _dft1(x.reshape(b, n1, n2, D_MODEL), mod, gain, m1, layer)
    out = _dft2_residual(y, x.reshape(b, n2, n1, D_MODEL), mod, t, cs, perm, w_o, layer, a)
    return out.reshape(b, s, D_MODEL)


def _run_trunk(x, mod, w):
    s = x.shape[1]
    cos, sin = _rope_tables(s)
    tables = _dft_tables(s)
    for l in range(DEPTH):
        gains = w["norm_gain"][l]
        x = _ffn(x, mod, gains[0:1], w["ffn_w_in"], w["ffn_w_out"], l, 0, 0)
        a = l // 2
        if l % 2 == 0:
            q, k4, vt = _qkv(x, mod, gains[1:2], w["attn_w_qkv"], w["q_gain"][a], w["k_gain"][a],
                             cos, sin, w["ones_bd"], l, a)
            o = _attention(q, k4, vt, w["score_bound"][a], l)
            x = _ffn(x, mod, gains[2:3], w["ffn_w_in"], w["ffn_w_out"], l, 1, 2,
                     mixed=o, w_proj=w["attn_w_o"], proj_layer=a)
        else:
            x = _fourier_mix(x, mod, gains[1:2], w["fnet_w_o"], tables, l, a)
            x = _ffn(x, mod, gains[2:3], w["ffn_w_in"], w["ffn_w_out"], l, 1, 2)
    return x


def kernel(x_prompt, x_sample, c_prompt, c_sample, norm_gain, ada_w, ada_b, ffn_w_in, ffn_w_out,
           attn_w_qkv, attn_q_gain, attn_k_gain, attn_w_o, fnet_w_o):
    head = np.arange(Q_DIM) // HEAD_DIM
    w = {
        "norm_gain": norm_gain,
        "ffn_w_in": ffn_w_in.astype(BF16),
        "ffn_w_out": ffn_w_out.astype(BF16),
        "attn_w_qkv": attn_w_qkv.astype(BF16),
        "attn_w_o": attn_w_o.astype(BF16),
        "fnet_w_o": fnet_w_o.astype(BF16),
        "q_gain": jnp.tile(attn_q_gain, (1, N_HEADS))[:, None, :],
        "k_gain": jnp.tile(attn_k_gain, (1, N_KV_HEADS))[:, None, :],
        "ones_bd": jnp.asarray(head[:, None] == head[None, :], BF16),
        "score_bound": (SCORE_BOUND_SLACK * math.sqrt(HEAD_DIM) * LOG2E
                        * jnp.max(jnp.abs(attn_q_gain), axis=1) * jnp.max(jnp.abs(attn_k_gain), axis=1)),
    }
    nb = x_prompt.shape[0]
    mod = _modulation(jnp.concatenate([c_prompt, c_sample], axis=0), ada_w, ada_b)
    y_prompt = _run_trunk(x_prompt, mod[:, :nb], w)
    y_sample = _run_trunk(x_sample, mod[:, nb:], w)
    return (y_prompt, y_sample)
```

```python
import functools
import math

import numpy as np
import jax
import jax.numpy as jnp
from jax import lax
from jax.experimental import pallas as pl
from jax.experimental.pallas import tpu as pltpu

F32 = jnp.float32
BF16 = jnp.bfloat16

D_MODEL = 1024
DEPTH = 4
N_HEADS = 16
N_KV_HEADS = 4
KV_GROUP = N_HEADS // N_KV_HEADS
HEAD_DIM = D_MODEL // N_HEADS
Q_DIM = N_HEADS * HEAD_DIM
KV_DIM = N_KV_HEADS * HEAD_DIM
QKV_DIM = Q_DIM + 2 * KV_DIM
GROUP_W = KV_GROUP * HEAD_DIM
ROPE_AXIS_DIM = HEAD_DIM // 2
ROPE_THETA = 10000.0
FNET_GROUPS = 8
FNET_GROUP_DIM = D_MODEL // FNET_GROUPS
D_FF = 2816
GRID_W = 64
N_SUBLAYERS = 3
N_MOD = 3
EPS = 1e-6
LOG2E = 1.4426950408889634

LANES = 128
VMEM_LIMIT = 48 * 1024 * 1024

FFN_TM = 512
TOK_TM = 512
QKV_SUB = 2
ATTN_TQ = 256
ATTN_BLOCKS = 8
ATTN_TK = 2048
ATTN_NOSHIFT_MAX_SCORE = 48.0
SCORE_BOUND_SLACK = 1.02
VT_ROWS = HEAD_DIM + 16
DFT_N2_MIN = 64
DFT_N1_MAX = 64
DFT_KB = 8
DFT_SB = 16


def _params(*sem):
    return pltpu.CompilerParams(dimension_semantics=sem, vmem_limit_bytes=VMEM_LIMIT)


def _dot(a, b):
    return jnp.dot(a, b, preferred_element_type=F32)


def _modnorm(x, gain, shift, scale):
    ms = jnp.mean(x * x, axis=-1, keepdims=True)
    return (x * lax.rsqrt(ms + EPS)) * (gain * (1.0 + scale)) + shift


def _mod_kernel(c_ref, w_ref, b_ref, o_ref):
    c = c_ref[...]
    a = (c * jax.nn.sigmoid(c)).astype(BF16)
    o_ref[...] = _dot(a, w_ref[...].astype(BF16)) + b_ref[...]


def _modulation(c, ada_w, ada_b):
    b = c.shape[0]
    n = N_SUBLAYERS * N_MOD
    out = pl.pallas_call(
        _mod_kernel,
        grid=(DEPTH, n),
        in_specs=[
            pl.BlockSpec((b, D_MODEL), lambda l, j: (0, 0)),
            pl.BlockSpec((None, D_MODEL, D_MODEL), lambda l, j: (l, 0, j)),
            pl.BlockSpec((None, 1, D_MODEL), lambda l, j: (l, 0, j)),
        ],
        out_specs=pl.BlockSpec((None, b, D_MODEL), lambda l, j: (l, 0, j)),
        out_shape=jax.ShapeDtypeStruct((DEPTH, b, n * D_MODEL), F32),
        compiler_params=_params("parallel", "parallel"),
        name="adaln_mod",
    )(c, ada_w, ada_b.reshape(DEPTH, 1, n * D_MODEL))
    return out.reshape(DEPTH, b, n, D_MODEL)


def _ffn_update(x, mod, gain, wg_ref, wu_ref, wo_ref, sub):
    h = _modnorm(x, gain, mod[3 * sub:3 * sub + 1], mod[3 * sub + 1:3 * sub + 2]).astype(BF16)
    g = _dot(h, wg_ref[...])
    u = _dot(h, wu_ref[...])
    a = (g * jax.nn.sigmoid(g) * u).astype(BF16)
    return x + (0.5 * mod[3 * sub + 2:3 * sub + 3]) * _dot(a, wo_ref[...])


def _ffn_kernel(x_ref, mod_ref, gain_ref, wg_ref, wu_ref, wo_ref, o_ref, *, sub):
    o_ref[...] = _ffn_update(x_ref[...], mod_ref[...], gain_ref[...], wg_ref, wu_ref, wo_ref, sub)


def _proj_ffn_kernel(x_ref, a_ref, wp_ref, mod_ref, gain_ref, wg_ref, wu_ref, wo_ref, o_ref, *, sub):
    mod = mod_ref[...]
    x = x_ref[...] + mod[3 * (sub - 1) + 2:3 * (sub - 1) + 3] * _dot(a_ref[...], wp_ref[...])
    o_ref[...] = _ffn_update(x, mod, gain_ref[...], wg_ref, wu_ref, wo_ref, sub)


def _resident(block_shape, index_map):
    return pl.BlockSpec(block_shape, index_map, pipeline_mode=pl.Buffered(1))


def _ffn(x, mod, gain, w_in, w_out, layer, which, sub, mixed=None, w_proj=None, proj_layer=None):
    b, s, _ = x.shape
    tm = min(FFN_TM, s)
    tok = pl.BlockSpec((None, tm, D_MODEL), lambda i, t: (i, t, 0))
    specs = [
        pl.BlockSpec((None, None, N_SUBLAYERS * N_MOD, D_MODEL), lambda i, t: (layer, i, 0, 0)),
        pl.BlockSpec((1, D_MODEL), lambda i, t: (0, 0)),
        _resident((None, None, D_MODEL, D_FF), lambda i, t: (layer, which, 0, 0)),
        _resident((None, None, D_MODEL, D_FF), lambda i, t: (layer, which, 0, 1)),
        _resident((None, None, D_FF, D_MODEL), lambda i, t: (layer, which, 0, 0)),
    ]
    args = (mod, gain, w_in, w_in, w_out)
    if mixed is None:
        body, specs, args = _ffn_kernel, [tok] + specs, (x,) + args
    else:
        proj = _resident((None, D_MODEL, D_MODEL), lambda i, t: (proj_layer, 0, 0))
        body, specs, args = _proj_ffn_kernel, [tok, tok, proj] + specs, (x, mixed, w_proj) + args
    return pl.pallas_call(
        functools.partial(body, sub=sub),
        grid=(b, s // tm),
        in_specs=specs,
        out_specs=tok,
        out_shape=jax.ShapeDtypeStruct(x.shape, F32),
        compiler_params=_params("parallel", "parallel"),
        name=f"ffn_l{layer}_{which}",
    )(*args)


def _head_rms(t, ones_bd, gain):
    ss = _dot((t * t).astype(BF16), ones_bd)
    return t * lax.rsqrt(ss * (1.0 / HEAD_DIM) + EPS) * gain


def _rope(t, cos, sin_signed):
    n = t.shape[1]
    lane = lax.broadcasted_iota(jnp.int32, t.shape, 1)
    nxt = pltpu.roll(t, n - 1, 1)
    prv = pltpu.roll(t, 1, 1)
    partner = jnp.where(lane % 2 == 0, nxt, prv)
    reps = n // LANES
    cosf = jnp.concatenate([cos] * reps, axis=1)
    sinf = jnp.concatenate([sin_signed] * reps, axis=1)
    return t * cosf + partner * sinf


def _repeat_heads(t):
    lane = lax.broadcasted_iota(jnp.int32, (t.shape[0], LANES), 1)
    low = lane < HEAD_DIM
    cols = []
    for j in range(KV_DIM // LANES):
        c = t[:, j * LANES:(j + 1) * LANES]
        sw = pltpu.roll(c, HEAD_DIM, 1)
        first = jnp.where(low, c, sw)
        second = jnp.where(low, sw, c)
        cols += [first] * (GROUP_W // LANES) + [second] * (GROUP_W // LANES)
    return jnp.concatenate(cols, axis=1)


def _qkv_kernel(x_ref, mod_ref, gain_ref, w_ref, qg_ref, kg_ref, cos_ref, sin_ref, bd_ref,
                q_ref, k_ref, vt_ref):
    mod = mod_ref[...]
    tm = x_ref.shape[0]
    half = tm // QKV_SUB
    for u in range(QKV_SUB):
        rows = slice(u * half, (u + 1) * half)
        h = _modnorm(x_ref[rows, :], gain_ref[...], mod[3:4], mod[4:5]).astype(BF16)
        qkv = _dot(h, w_ref[...])
        q = qkv[:, :Q_DIM]
        k = qkv[:, Q_DIM:Q_DIM + KV_DIM]
        v = qkv[:, Q_DIM + KV_DIM:]
        cos = cos_ref[rows, :]
        sin = sin_ref[rows, :]
        q = _rope(_head_rms(q, bd_ref[...], qg_ref[...]), cos, sin)
        k = _rope(_head_rms(k, bd_ref[:KV_DIM, :KV_DIM], kg_ref[...]), cos, sin)
        q_ref[rows, :] = (q * (LOG2E / math.sqrt(HEAD_DIM))).astype(BF16)
        k_ref[rows, :] = _repeat_heads(k).astype(BF16)
        v_t = v.T
        ones = jnp.ones((VT_ROWS - HEAD_DIM, half), BF16)
        for j in range(N_KV_HEADS):
            vt_ref[j, :HEAD_DIM, rows] = v_t[j * HEAD_DIM:(j + 1) * HEAD_DIM].astype(BF16)
            vt_ref[j, HEAD_DIM:, rows] = ones


def _rope_tables(s):
    rows = s // GRID_W
    r = jnp.repeat(jnp.arange(rows), GRID_W).astype(F32)
    c = jnp.tile(jnp.arange(GRID_W), rows).astype(F32)
    inv = ROPE_THETA ** (-jnp.arange(0, ROPE_AXIS_DIM, 2, dtype=F32) / ROPE_AXIS_DIM)
    ang = jnp.concatenate([r[:, None] * inv, c[:, None] * inv], axis=-1)
    cos = jnp.repeat(jnp.cos(ang), 2, axis=-1)
    sin = jnp.repeat(jnp.sin(ang), 2, axis=-1)
    sign = jnp.tile(jnp.array([-1.0, 1.0], F32), HEAD_DIM // 2)
    reps = LANES // HEAD_DIM
    return jnp.tile(cos, (1, reps)), jnp.tile(sin * sign, (1, reps))


def _qkv(x, mod, gain, w_qkv, q_gain, k_gain, cos, sin, ones_bd, layer, a):
    b, s, _ = x.shape
    tm = min(TOK_TM, s)
    tok = lambda i, t: (i, t, 0)
    const2 = lambda i, t: (0, 0)
    shape = jax.ShapeDtypeStruct((b, s, D_MODEL), BF16)
    tk = min(ATTN_TK, s)
    per = tk // tm
    vt_shape = jax.ShapeDtypeStruct((b, N_KV_HEADS, s // tk, VT_ROWS, tk), BF16)
    vt_spec = pl.BlockSpec((None, N_KV_HEADS, None, VT_ROWS, tm), lambda i, t: (i, 0, t // per, 0, t % per))
    return pl.pallas_call(
        _qkv_kernel,
        grid=(b, s // tm),
        in_specs=[
            pl.BlockSpec((None, tm, D_MODEL), tok),
            pl.BlockSpec((None, None, N_SUBLAYERS * N_MOD, D_MODEL), lambda i, t: (layer, i, 0, 0)),
            pl.BlockSpec((1, D_MODEL), const2),
            pl.BlockSpec((None, D_MODEL, QKV_DIM), lambda i, t: (a, 0, 0)),
            pl.BlockSpec((1, Q_DIM), const2),
            pl.BlockSpec((1, KV_DIM), const2),
            pl.BlockSpec((tm, LANES), lambda i, t: (t, 0)),
            pl.BlockSpec((tm, LANES), lambda i, t: (t, 0)),
            pl.BlockSpec((Q_DIM, Q_DIM), const2),
        ],
        out_specs=[pl.BlockSpec((None, tm, D_MODEL), tok)] * 2 + [vt_spec],
        out_shape=[shape, shape, vt_shape],
        compiler_params=_params("parallel", "parallel"),
        name=f"qkv_l{layer}",
    )(x, mod, gain, w_qkv, q_gain, k_gain, cos, sin, ones_bd)


def _attn_kernel(q_ref, k_ref, vt_ref, o_ref, *, tq, nsub, tk, nk, shift):
    head = lax.broadcasted_iota(jnp.int32, (tq, GROUP_W), 1) // HEAD_DIM
    w = KV_GROUP * tq
    qs = []
    for u in range(nsub):
        q = q_ref[u * tq:(u + 1) * tq, :]
        zero = jnp.zeros_like(q)
        qs.append(jnp.concatenate([jnp.where(head == g, q, zero) for g in range(KV_GROUP)], axis=0))

    def scores(c, u):
        kc = k_ref[pl.ds(pl.multiple_of(c * tk, tk), tk), :]
        return lax.dot_general(kc, qs[u], (((1,), (1,)), ((), ())), preferred_element_type=F32)

    def chunk_shifted(c, carry):
        out = []
        for u, (m, acc) in enumerate(carry):
            st = scores(c, u)
            m_new = jnp.maximum(m, jnp.max(st, axis=0, keepdims=True))
            p = jnp.exp2(st - m_new).astype(BF16)
            out.append((m_new, jnp.exp2(m - m_new) * acc + _dot(vt_ref[c], p)))
        return tuple(out)

    def chunk_plain(c, accs):
        return tuple(acc + _dot(vt_ref[c], jnp.exp2(scores(c, u)).astype(BF16)) for u, acc in enumerate(accs))

    acc0 = jnp.zeros((VT_ROWS, w), F32)
    if shift:
        carry = lax.fori_loop(0, nk, chunk_shifted, ((jnp.full((1, w), -1e30, F32), acc0),) * nsub)
        accs = [acc for _, acc in carry]
    else:
        accs = lax.fori_loop(0, nk, chunk_plain, (acc0,) * nsub, unroll=True)
    for u, acc in enumerate(accs):
        out = acc[:HEAD_DIM] / acc[HEAD_DIM:HEAD_DIM + 1]
        o_t = jnp.concatenate([out[:, g * tq:(g + 1) * tq] for g in range(KV_GROUP)], axis=0)
        o_ref[u * tq:(u + 1) * tq, :] = o_t.T.astype(BF16)


def _attention(q, k4, vt, score_bound, layer):
    b, s, _ = q.shape
    nk, tk = vt.shape[2], vt.shape[4]
    tq = min(ATTN_TQ, s)
    nsub = min(ATTN_BLOCKS // nk, s // tq)
    rows = nsub * tq

    def call(shift):
        return pl.pallas_call(
            functools.partial(_attn_kernel, tq=tq, nsub=nsub, tk=tk, nk=nk, shift=shift),
            grid=(b, N_KV_HEADS, s // rows),
            in_specs=[
                pl.BlockSpec((None, rows, GROUP_W), lambda i, j, t: (i, t, j)),
                pl.BlockSpec((None, s, GROUP_W), lambda i, j, t: (i, 0, j)),
                pl.BlockSpec((None, None, nk, VT_ROWS, tk), lambda i, j, t: (i, j, 0, 0, 0)),
            ],
            out_specs=pl.BlockSpec((None, rows, GROUP_W), lambda i, j, t: (i, t, j)),
            out_shape=jax.ShapeDtypeStruct((b, s, D_MODEL), BF16),
            compiler_params=_params("parallel", "parallel", "parallel"),
            name=f"attn_l{layer}_{'shifted' if shift else 'plain'}",
        )

    return lax.cond(score_bound <= ATTN_NOSHIFT_MAX_SCORE,
                    lambda *a: call(False)(*a), lambda *a: call(True)(*a), q, k4, vt)


def _dft_split(s):
    n2 = max(DFT_N2_MIN, s // DFT_N1_MAX)
    return s // n2, n2


def _dft_tables(s):
    n1, n2 = _dft_split(s)
    k1 = np.arange(n1)
    ang1 = 2.0 * np.pi * ((k1[:, None] * k1[None, :]) % n1) / n1
    m1 = np.concatenate([np.cos(ang1), -np.sin(ang1)], axis=0)
    m1 = np.kron(m1, np.eye(DFT_SB))
    src = (np.arange(DFT_KB)[None, :] * n2 + np.arange(n2)[:, None]).reshape(-1)
    perm = np.zeros((DFT_KB * n2, DFT_KB * n2))
    perm[np.arange(DFT_KB * n2), src] = 1.0
    s2 = np.arange(n2)
    num = (s2[None, None, :] * k1[:, None, None] + s2[None, None, :] * s2[None, :, None] * n1) % s
    psi = 2.0 * np.pi * num / s
    tr, ti = np.cos(psi), -np.sin(psi)
    t = np.concatenate([np.concatenate([tr, -ti], axis=2), np.concatenate([ti, tr], axis=2)], axis=1)
    c = np.arange(FNET_GROUP_DIM)
    phi = 2.0 * np.pi * ((c[:, None] * c[None, :]) % FNET_GROUP_DIM) / FNET_GROUP_DIM
    cs = np.concatenate([np.cos(phi), np.sin(phi)], axis=0)
    return jnp.asarray(m1, BF16), jnp.asarray(t, BF16), jnp.asarray(cs, BF16), jnp.asarray(perm, BF16)


def _dft1_kernel(x_ref, mod_ref, gain_ref, m_ref, y_ref, *, n1, sb):
    mod = mod_ref[...]
    x = x_ref[...].reshape(n1 * sb, D_MODEL)
    h = _modnorm(x, gain_ref[...], mod[3:4], mod[4:5]).astype(BF16)
    y_ref[...] = _dot(m_ref[...], h).astype(BF16).reshape(2, n1, sb, D_MODEL)


def _dft1(x, mod, gain, m1, layer):
    b, n1, n2, _ = x.shape
    sb = DFT_SB
    return pl.pallas_call(
        functools.partial(_dft1_kernel, n1=n1, sb=sb),
        grid=(b, n2 // sb),
        in_specs=[
            pl.BlockSpec((None, n1, sb, D_MODEL), lambda i, t: (i, 0, t, 0)),
            pl.BlockSpec((None, None, N_SUBLAYERS * N_MOD, D_MODEL), lambda i, t: (layer, i, 0, 0)),
            pl.BlockSpec((1, D_MODEL), lambda i, t: (0, 0)),
            _resident((2 * n1 * sb, n1 * sb), lambda i, t: (0, 0)),
        ],
        out_specs=pl.BlockSpec((None, 2, n1, sb, D_MODEL), lambda i, t: (i, 0, 0, t, 0)),
        out_shape=jax.ShapeDtypeStruct((b, 2, n1, n2, D_MODEL), BF16),
        compiler_params=_params("parallel", "parallel"),
        name=f"fnet_dft1_l{layer}",
    )(x, mod, gain, m1)


def _dft2_kernel(t_ref, y_ref, x_ref, mod_ref, cs_ref, perm_ref, w_ref, out_ref, u_ref, *, kb, n2, inv_norm):
    for i in range(kb):
        y = jnp.concatenate([y_ref[0, i], y_ref[1, i]], axis=0)
        u = _dot(t_ref[i], y).astype(BF16)
        u_ref[0, i * n2:(i + 1) * n2, :] = u[:n2]
        u_ref[1, i * n2:(i + 1) * n2, :] = u[n2:]
    ur = u_ref[0]
    ui = u_ref[1]
    cs = cs_ref[...]
    parts = []
    for g in range(FNET_GROUPS):
        cols = slice(g * FNET_GROUP_DIM, (g + 1) * FNET_GROUP_DIM)
        parts.append(_dot(jnp.concatenate([ur[:, cols], ui[:, cols]], axis=1), cs))
    f = (jnp.concatenate(parts, axis=1) * inv_norm).astype(BF16)
    f = _dot(perm_ref[...], f).astype(BF16)
    r = mod_ref[...][5:6] * _dot(f, w_ref[...])
    out_ref[...] = x_ref[...] + r.reshape(n2, kb, D_MODEL)


def _dft2_residual(y, x, mod, t, cs, perm, w_o, layer, a):
    b, _, n1, n2, _ = y.shape
    kb = DFT_KB
    inv_norm = 1.0 / math.sqrt(n1 * n2 * FNET_GROUP_DIM)
    return pl.pallas_call(
        functools.partial(_dft2_kernel, kb=kb, n2=n2, inv_norm=inv_norm),
        grid=(b, n1 // kb),
        in_specs=[
            pl.BlockSpec((kb, 2 * n2, 2 * n2), lambda i, t: (t, 0, 0)),
            pl.BlockSpec((None, 2, kb, n2, D_MODEL), lambda i, t: (i, 0, t, 0, 0)),
            pl.BlockSpec((None, n2, kb, D_MODEL), lambda i, t: (i, 0, t, 0)),
            pl.BlockSpec((None, None, N_SUBLAYERS * N_MOD, D_MODEL), lambda i, t: (layer, i, 0, 0)),
            pl.BlockSpec((2 * FNET_GROUP_DIM, FNET_GROUP_DIM), lambda i, t: (0, 0)),
            pl.BlockSpec((kb * n2, kb * n2), lambda i, t: (0, 0)),
            pl.BlockSpec((None, D_MODEL, D_MODEL), lambda i, t: (a, 0, 0)),
        ],
        out_specs=pl.BlockSpec((None, n2, kb, D_MODEL), lambda i, t: (i, 0, t, 0)),
        out_shape=jax.ShapeDtypeStruct(x.shape, F32),
        scratch_shapes=[pltpu.VMEM((2, kb * n2, D_MODEL), BF16)],
        compiler_params=_params("parallel", "parallel"),
        name=f"fnet_dft2_l{layer}",
    )(t, y, x, mod, cs, perm, w_o)


def _fourier_mix(x, mod, gain, w_o, tables, layer, a):
    b, s, _ = x.shape
    m1, t, cs, perm = tables
    n1, n2 = _dft_split(s)
    y = _dft1(x.reshape(b, n1, n2, D_MODEL), mod, gain, m1, layer)
    out = _dft2_residual(y, x.reshape(b, n2, n1, D_MODEL), mod, t, cs, perm, w_o, layer, a)
    return out.reshape(b, s, D_MODEL)


def _run_trunk(x, mod, w):
    s = x.shape[1]
    cos, sin = _rope_tables(s)
    tables = _dft_tables(s)
    for l in range(DEPTH):
        gains = w["norm_gain"][l]
        x = _ffn(x, mod, gains[0:1], w["ffn_w_in"], w["ffn_w_out"], l, 0, 0)
        a = l // 2
        if l % 2 == 0:
            q, k4, vt = _qkv(x, mod, gains[1:2], w["attn_w_qkv"], w["q_gain"][a], w["k_gain"][a],
                             cos, sin, w["ones_bd"], l, a)
            o = _attention(q, k4, vt, w["score_bound"][a], l)
            x = _ffn(x, mod, gains[2:3], w["ffn_w_in"], w["ffn_w_out"], l, 1, 2,
                     mixed=o, w_proj=w["attn_w_o"], proj_layer=a)
        else:
            x = _fourier_mix(x, mod, gains[1:2], w["fnet_w_o"], tables, l, a)
            x = _ffn(x, mod, gains[2:3], w["ffn_w_in"], w["ffn_w_out"], l, 1, 2)
    return x


def kernel(x_prompt, x_sample, c_prompt, c_sample, norm_gain, ada_w, ada_b, ffn_w_in, ffn_w_out,
           attn_w_qkv, attn_q_gain, attn_k_gain, attn_w_o, fnet_w_o):
    head = np.arange(Q_DIM) // HEAD_DIM
    w = {
        "norm_gain": norm_gain,
        "ffn_w_in": ffn_w_in.astype(BF16),
        "ffn_w_out": ffn_w_out.astype(BF16),
        "attn_w_qkv": attn_w_qkv.astype(BF16),
        "attn_w_o": attn_w_o.astype(BF16),
        "fnet_w_o": fnet_w_o.astype(BF16),
        "q_gain": jnp.tile(attn_q_gain, (1, N_HEADS))[:, None, :],
        "k_gain": jnp.tile(attn_k_gain, (1, N_KV_HEADS))[:, None, :],
        "ones_bd": jnp.asarray(head[:, None] == head[None, :], BF16),
        "score_bound": (SCORE_BOUND_SLACK * math.sqrt(HEAD_DIM) * LOG2E
                        * jnp.max(jnp.abs(attn_q_gain), axis=1) * jnp.max(jnp.abs(attn_k_gain), axis=1)),
    }
    nb = x_prompt.shape[0]
    mod = _modulation(jnp.concatenate([c_prompt, c_sample], axis=0), ada_w, ada_b)
    y_prompt = _run_trunk(x_prompt, mod[:, :nb], w)
    y_sample = _run_trunk(x_sample, mod[:, nb:], w)
    return (y_prompt, y_sample)
```

```python
import functools
import math

import numpy as np
import jax
import jax.numpy as jnp
from jax import lax
from jax.experimental import pallas as pl
from jax.experimental.pallas import tpu as pltpu

F32 = jnp.float32
BF16 = jnp.bfloat16

D_MODEL = 1024
DEPTH = 4
N_HEADS = 16
N_KV_HEADS = 4
KV_GROUP = N_HEADS // N_KV_HEADS
HEAD_DIM = D_MODEL // N_HEADS
Q_DIM = N_HEADS * HEAD_DIM
KV_DIM = N_KV_HEADS * HEAD_DIM
QKV_DIM = Q_DIM + 2 * KV_DIM
GROUP_W = KV_GROUP * HEAD_DIM
ROPE_AXIS_DIM = HEAD_DIM // 2
ROPE_THETA = 10000.0
FNET_GROUPS = 8
FNET_GROUP_DIM = D_MODEL // FNET_GROUPS
D_FF = 2816
GRID_W = 64
N_SUBLAYERS = 3
N_MOD = 3
EPS = 1e-6
LOG2E = 1.4426950408889634

LANES = 128
VMEM_LIMIT = 48 * 1024 * 1024
ATTN_VMEM_LIMIT = 56 * 1024 * 1024

FFN_TM = 512
TOK_TM = 512
QKV_SUB = 2
ATTN_TQ = 256
ATTN_BLOCKS = 16
ATTN_TK = 2048
ATTN_NOSHIFT_MAX_SCORE = 48.0
SCORE_BOUND_SLACK = 1.02
VT_ROWS = HEAD_DIM + 16
DFT_N2_MIN = 64
DFT_N1_MAX = 64
DFT_KB = 8
DFT_SB = 16


def _params(*sem, vmem_limit=VMEM_LIMIT):
    return pltpu.CompilerParams(dimension_semantics=sem, vmem_limit_bytes=vmem_limit)


def _dot(a, b):
    return jnp.dot(a, b, preferred_element_type=F32)


def _modnorm(x, gain, shift, scale):
    ms = jnp.mean(x * x, axis=-1, keepdims=True)
    return (x * lax.rsqrt(ms + EPS)) * (gain * (1.0 + scale)) + shift


def _mod_kernel(c_ref, w_ref, b_ref, o_ref):
    c = c_ref[...]
    a = (c * jax.nn.sigmoid(c)).astype(BF16)
    o_ref[...] = _dot(a, w_ref[...].astype(BF16)) + b_ref[...]


def _modulation(c, ada_w, ada_b):
    b = c.shape[0]
    n = N_SUBLAYERS * N_MOD
    out = pl.pallas_call(
        _mod_kernel,
        grid=(DEPTH, n),
        in_specs=[
            pl.BlockSpec((b, D_MODEL), lambda l, j: (0, 0)),
            pl.BlockSpec((None, D_MODEL, D_MODEL), lambda l, j: (l, 0, j)),
            pl.BlockSpec((None, 1, D_MODEL), lambda l, j: (l, 0, j)),
        ],
        out_specs=pl.BlockSpec((None, b, D_MODEL), lambda l, j: (l, 0, j)),
        out_shape=jax.ShapeDtypeStruct((DEPTH, b, n * D_MODEL), F32),
        compiler_params=_params("parallel", "parallel"),
        name="adaln_mod",
    )(c, ada_w, ada_b.reshape(DEPTH, 1, n * D_MODEL))
    return out.reshape(DEPTH, b, n, D_MODEL)


def _ffn_update(x, mod, gain, wg_ref, wu_ref, wo_ref, sub):
    h = _modnorm(x, gain, mod[3 * sub:3 * sub + 1], mod[3 * sub + 1:3 * sub + 2]).astype(BF16)
    g = _dot(h, wg_ref[...])
    u = _dot(h, wu_ref[...])
    a = (g * jax.nn.sigmoid(g) * u).astype(BF16)
    return x + (0.5 * mod[3 * sub + 2:3 * sub + 3]) * _dot(a, wo_ref[...])


def _ffn_kernel(x_ref, mod_ref, gain_ref, wg_ref, wu_ref, wo_ref, o_ref, *, sub):
    o_ref[...] = _ffn_update(x_ref[...], mod_ref[...], gain_ref[...], wg_ref, wu_ref, wo_ref, sub)


def _proj_ffn_kernel(x_ref, a_ref, wp_ref, mod_ref, gain_ref, wg_ref, wu_ref, wo_ref, o_ref, *, sub):
    mod = mod_ref[...]
    x = x_ref[...] + mod[3 * (sub - 1) + 2:3 * (sub - 1) + 3] * _dot(a_ref[...], wp_ref[...])
    o_ref[...] = _ffn_update(x, mod, gain_ref[...], wg_ref, wu_ref, wo_ref, sub)


def _resident(block_shape, index_map):
    return pl.BlockSpec(block_shape, index_map, pipeline_mode=pl.Buffered(1))


def _ffn(x, mod, gain, w_in, w_out, layer, which, sub, mixed=None, w_proj=None, proj_layer=None):
    b, s, _ = x.shape
    tm = min(FFN_TM, s)
    tok = pl.BlockSpec((None, tm, D_MODEL), lambda i, t: (i, t, 0))
    specs = [
        pl.BlockSpec((None, None, N_SUBLAYERS * N_MOD, D_MODEL), lambda i, t: (layer, i, 0, 0)),
        pl.BlockSpec((1, D_MODEL), lambda i, t: (0, 0)),
        _resident((None, None, D_MODEL, D_FF), lambda i, t: (layer, which, 0, 0)),
        _resident((None, None, D_MODEL, D_FF), lambda i, t: (layer, which, 0, 1)),
        _resident((None, None, D_FF, D_MODEL), lambda i, t: (layer, which, 0, 0)),
    ]
    args = (mod, gain, w_in, w_in, w_out)
    if mixed is None:
        body, specs, args = _ffn_kernel, [tok] + specs, (x,) + args
    else:
        proj = _resident((None, D_MODEL, D_MODEL), lambda i, t: (proj_layer, 0, 0))
        body, specs, args = _proj_ffn_kernel, [tok, tok, proj] + specs, (x, mixed, w_proj) + args
    return pl.pallas_call(
        functools.partial(body, sub=sub),
        grid=(b, s // tm),
        in_specs=specs,
        out_specs=tok,
        out_shape=jax.ShapeDtypeStruct(x.shape, F32),
        compiler_params=_params("parallel", "parallel"),
        name=f"ffn_l{layer}_{which}",
    )(*args)


def _head_rms(t, ones_bd, gain):
    ss = _dot((t * t).astype(BF16), ones_bd)
    return t * lax.rsqrt(ss * (1.0 / HEAD_DIM) + EPS) * gain


def _rope(t, cos, sin_signed):
    n = t.shape[1]
    lane = lax.broadcasted_iota(jnp.int32, t.shape, 1)
    nxt = pltpu.roll(t, n - 1, 1)
    prv = pltpu.roll(t, 1, 1)
    partner = jnp.where(lane % 2 == 0, nxt, prv)
    reps = n // LANES
    cosf = jnp.concatenate([cos] * reps, axis=1)
    sinf = jnp.concatenate([sin_signed] * reps, axis=1)
    return t * cosf + partner * sinf


def _repeat_heads(t):
    lane = lax.broadcasted_iota(jnp.int32, (t.shape[0], LANES), 1)
    low = lane < HEAD_DIM
    cols = []
    for j in range(KV_DIM // LANES):
        c = t[:, j * LANES:(j + 1) * LANES]
        sw = pltpu.roll(c, HEAD_DIM, 1)
        first = jnp.where(low, c, sw)
        second = jnp.where(low, sw, c)
        cols += [first] * (GROUP_W // LANES) + [second] * (GROUP_W // LANES)
    return jnp.concatenate(cols, axis=1)


def _qkv_kernel(x_ref, mod_ref, gain_ref, w_ref, qg_ref, kg_ref, cos_ref, sin_ref, bd_ref,
                q_ref, k_ref, vt_ref):
    mod = mod_ref[...]
    tm = x_ref.shape[0]
    half = tm // QKV_SUB
    for u in range(QKV_SUB):
        rows = slice(u * half, (u + 1) * half)
        h = _modnorm(x_ref[rows, :], gain_ref[...], mod[3:4], mod[4:5]).astype(BF16)
        qkv = _dot(h, w_ref[...])
        q = qkv[:, :Q_DIM]
        k = qkv[:, Q_DIM:Q_DIM + KV_DIM]
        v = qkv[:, Q_DIM + KV_DIM:]
        cos = cos_ref[rows, :]
        sin = sin_ref[rows, :]
        q = _rope(_head_rms(q, bd_ref[...], qg_ref[...]), cos, sin)
        k = _rope(_head_rms(k, bd_ref[:KV_DIM, :KV_DIM], kg_ref[...]), cos, sin)
        q_ref[rows, :] = (q * (LOG2E / math.sqrt(HEAD_DIM))).astype(BF16)
        k_ref[rows, :] = _repeat_heads(k).astype(BF16)
        v_t = v.T
        ones = jnp.ones((VT_ROWS - HEAD_DIM, half), BF16)
        for j in range(N_KV_HEADS):
            vt_ref[j, :HEAD_DIM, rows] = v_t[j * HEAD_DIM:(j + 1) * HEAD_DIM].astype(BF16)
            vt_ref[j, HEAD_DIM:, rows] = ones


def _rope_tables(s):
    rows = s // GRID_W
    r = jnp.repeat(jnp.arange(rows), GRID_W).astype(F32)
    c = jnp.tile(jnp.arange(GRID_W), rows).astype(F32)
    inv = ROPE_THETA ** (-jnp.arange(0, ROPE_AXIS_DIM, 2, dtype=F32) / ROPE_AXIS_DIM)
    ang = jnp.concatenate([r[:, None] * inv, c[:, None] * inv], axis=-1)
    cos = jnp.repeat(jnp.cos(ang), 2, axis=-1)
    sin = jnp.repeat(jnp.sin(ang), 2, axis=-1)
    sign = jnp.tile(jnp.array([-1.0, 1.0], F32), HEAD_DIM // 2)
    reps = LANES // HEAD_DIM
    return jnp.tile(cos, (1, reps)), jnp.tile(sin * sign, (1, reps))


def _qkv(x, mod, gain, w_qkv, q_gain, k_gain, cos, sin, ones_bd, layer, a):
    b, s, _ = x.shape
    tm = min(TOK_TM, s)
    tok = lambda i, t: (i, t, 0)
    const2 = lambda i, t: (0, 0)
    shape = jax.ShapeDtypeStruct((b, s, D_MODEL), BF16)
    tk = min(ATTN_TK, s)
    per = tk // tm
    vt_shape = jax.ShapeDtypeStruct((b, N_KV_HEADS, s // tk, VT_ROWS, tk), BF16)
    vt_spec = pl.BlockSpec((None, N_KV_HEADS, None, VT_ROWS, tm), lambda i, t: (i, 0, t // per, 0, t % per))
    return pl.pallas_call(
        _qkv_kernel,
        grid=(b, s // tm),
        in_specs=[
            pl.BlockSpec((None, tm, D_MODEL), tok),
            pl.BlockSpec((None, None, N_SUBLAYERS * N_MOD, D_MODEL), lambda i, t: (layer, i, 0, 0)),
            pl.BlockSpec((1, D_MODEL), const2),
            pl.BlockSpec((None, D_MODEL, QKV_DIM), lambda i, t: (a, 0, 0)),
            pl.BlockSpec((1, Q_DIM), const2),
            pl.BlockSpec((1, KV_DIM), const2),
            pl.BlockSpec((tm, LANES), lambda i, t: (t, 0)),
            pl.BlockSpec((tm, LANES), lambda i, t: (t, 0)),
            pl.BlockSpec((Q_DIM, Q_DIM), const2),
        ],
        out_specs=[pl.BlockSpec((None, tm, D_MODEL), tok)] * 2 + [vt_spec],
        out_shape=[shape, shape, vt_shape],
        compiler_params=_params("parallel", "parallel"),
        name=f"qkv_l{layer}",
    )(x, mod, gain, w_qkv, q_gain, k_gain, cos, sin, ones_bd)


def _attn_kernel(q_ref, k_ref, vt_ref, o_ref, *, tq, nsub, tk, nk, shift):
    head = lax.broadcasted_iota(jnp.int32, (tq, GROUP_W), 1) // HEAD_DIM
    w = KV_GROUP * tq
    qs = []
    for u in range(nsub):
        q = q_ref[u * tq:(u + 1) * tq, :]
        zero = jnp.zeros_like(q)
        qs.append(jnp.concatenate([jnp.where(head == g, q, zero) for g in range(KV_GROUP)], axis=0))

    def scores(c, u):
        kc = k_ref[pl.ds(pl.multiple_of(c * tk, tk), tk), :]
        return lax.dot_general(kc, qs[u], (((1,), (1,)), ((), ())), preferred_element_type=F32)

    def chunk_shifted(c, carry):
        out = []
        for u, (m, acc) in enumerate(carry):
            st = scores(c, u)
            m_new = jnp.maximum(m, jnp.max(st, axis=0, keepdims=True))
            p = jnp.exp2(st - m_new).astype(BF16)
            out.append((m_new, jnp.exp2(m - m_new) * acc + _dot(vt_ref[c], p)))
        return tuple(out)

    def chunk_plain(c, accs):
        return tuple(acc + _dot(vt_ref[c], jnp.exp2(scores(c, u)).astype(BF16)) for u, acc in enumerate(accs))

    acc0 = jnp.zeros((VT_ROWS, w), F32)
    if shift:
        carry = lax.fori_loop(0, nk, chunk_shifted, ((jnp.full((1, w), -1e30, F32), acc0),) * nsub)
        accs = [acc for _, acc in carry]
    else:
        accs = lax.fori_loop(0, nk, chunk_plain, (acc0,) * nsub, unroll=True)
    for u, acc in enumerate(accs):
        out = acc[:HEAD_DIM] / acc[HEAD_DIM:HEAD_DIM + 1]
        o_t = jnp.concatenate([out[:, g * tq:(g + 1) * tq] for g in range(KV_GROUP)], axis=0)
        o_ref[u * tq:(u + 1) * tq, :] = o_t.T.astype(BF16)


def _attention(q, k4, vt, score_bound, layer):
    b, s, _ = q.shape
    nk, tk = vt.shape[2], vt.shape[4]
    tq = min(ATTN_TQ, s)
    nsub = min(ATTN_BLOCKS // nk, s // tq)
    rows = nsub * tq

    def call(shift):
        return pl.pallas_call(
            functools.partial(_attn_kernel, tq=tq, nsub=nsub, tk=tk, nk=nk, shift=shift),
            grid=(b, N_KV_HEADS, s // rows),
            in_specs=[
                pl.BlockSpec((None, rows, GROUP_W), lambda i, j, t: (i, t, j)),
                pl.BlockSpec((None, s, GROUP_W), lambda i, j, t: (i, 0, j)),
                pl.BlockSpec((None, None, nk, VT_ROWS, tk), lambda i, j, t: (i, j, 0, 0, 0)),
            ],
            out_specs=pl.BlockSpec((None, rows, GROUP_W), lambda i, j, t: (i, t, j)),
            out_shape=jax.ShapeDtypeStruct((b, s, D_MODEL), BF16),
            compiler_params=_params("parallel", "parallel", "parallel", vmem_limit=ATTN_VMEM_LIMIT),
            name=f"attn_l{layer}_{'shifted' if shift else 'plain'}",
        )

    return lax.cond(score_bound <= ATTN_NOSHIFT_MAX_SCORE,
                    lambda *a: call(False)(*a), lambda *a: call(True)(*a), q, k4, vt)


def _dft_split(s):
    n2 = max(DFT_N2_MIN, s // DFT_N1_MAX)
    return s // n2, n2


def _dft_tables(s):
    n1, n2 = _dft_split(s)
    k1 = np.arange(n1)
    ang1 = 2.0 * np.pi * ((k1[:, None] * k1[None, :]) % n1) / n1
    m1 = np.concatenate([np.cos(ang1), -np.sin(ang1)], axis=0)
    m1 = np.kron(m1, np.eye(DFT_SB))
    src = (np.arange(DFT_KB)[None, :] * n2 + np.arange(n2)[:, None]).reshape(-1)
    perm = np.zeros((DFT_KB * n2, DFT_KB * n2))
    perm[np.arange(DFT_KB * n2), src] = 1.0
    s2 = np.arange(n2)
    num = (s2[None, None, :] * k1[:, None, None] + s2[None, None, :] * s2[None, :, None] * n1) % s
    psi = 2.0 * np.pi * num / s
    tr, ti = np.cos(psi), -np.sin(psi)
    t = np.concatenate([np.concatenate([tr, -ti], axis=2), np.concatenate([ti, tr], axis=2)], axis=1)
    c = np.arange(FNET_GROUP_DIM)
    phi = 2.0 * np.pi * ((c[:, None] * c[None, :]) % FNET_GROUP_DIM) / FNET_GROUP_DIM
    cs = np.concatenate([np.cos(phi), np.sin(phi)], axis=0)
    return jnp.asarray(m1, BF16), jnp.asarray(t, BF16), jnp.asarray(cs, BF16), jnp.asarray(perm, BF16)


def _dft1_kernel(x_ref, mod_ref, gain_ref, m_ref, y_ref, *, n1, sb):
    mod = mod_ref[...]
    x = x_ref[...].reshape(n1 * sb, D_MODEL)
    h = _modnorm(x, gain_ref[...], mod[3:4], mod[4:5]).astype(BF16)
    y_ref[...] = _dot(m_ref[...], h).astype(BF16).reshape(2, n1, sb, D_MODEL)


def _dft1(x, mod, gain, m1, layer):
    b, n1, n2, _ = x.shape
    sb = DFT_SB
    return pl.pallas_call(
        functools.partial(_dft1_kernel, n1=n1, sb=sb),
        grid=(b, n2 // sb),
        in_specs=[
            pl.BlockSpec((None, n1, sb, D_MODEL), lambda i, t: (i, 0, t, 0)),
            pl.BlockSpec((None, None, N_SUBLAYERS * N_MOD, D_MODEL), lambda i, t: (layer, i, 0, 0)),
            pl.BlockSpec((1, D_MODEL), lambda i, t: (0, 0)),
            _resident((2 * n1 * sb, n1 * sb), lambda i, t: (0, 0)),
        ],
        out_specs=pl.BlockSpec((None, 2, n1, sb, D_MODEL), lambda i, t: (i, 0, 0, t, 0)),
        out_shape=jax.ShapeDtypeStruct((b, 2, n1, n2, D_MODEL), BF16),
        compiler_params=_params("parallel", "parallel"),
        name=f"fnet_dft1_l{layer}",
    )(x, mod, gain, m1)


def _dft2_kernel(t_ref, y_ref, x_ref, mod_ref, cs_ref, perm_ref, w_ref, out_ref, u_ref, *, kb, n2, inv_norm):
    for i in range(kb):
        y = jnp.concatenate([y_ref[0, i], y_ref[1, i]], axis=0)
        u = _dot(t_ref[i], y).astype(BF16)
        u_ref[0, i * n2:(i + 1) * n2, :] = u[:n2]
        u_ref[1, i * n2:(i + 1) * n2, :] = u[n2:]
    ur = u_ref[0]
    ui = u_ref[1]
    cs = cs_ref[...]
    parts = []
    for g in range(FNET_GROUPS):
        cols = slice(g * FNET_GROUP_DIM, (g + 1) * FNET_GROUP_DIM)
        parts.append(_dot(jnp.concatenate([ur[:, cols], ui[:, cols]], axis=1), cs))
    f = (jnp.concatenate(parts, axis=1) * inv_norm).astype(BF16)
    f = _dot(perm_ref[...], f).astype(BF16)
    r = mod_ref[...][5:6] * _dot(f, w_ref[...])
    out_ref[...] = x_ref[...] + r.reshape(n2, kb, D_MODEL)


def _dft2_residual(y, x, mod, t, cs, perm, w_o, layer, a):
    b, _, n1, n2, _ = y.shape
    kb = DFT_KB
    inv_norm = 1.0 / math.sqrt(n1 * n2 * FNET_GROUP_DIM)
    return pl.pallas_call(
        functools.partial(_dft2_kernel, kb=kb, n2=n2, inv_norm=inv_norm),
        grid=(b, n1 // kb),
        in_specs=[
            pl.BlockSpec((kb, 2 * n2, 2 * n2), lambda i, t: (t, 0, 0)),
            pl.BlockSpec((None, 2, kb, n2, D_MODEL), lambda i, t: (i, 0, t, 0, 0)),
            pl.BlockSpec((None, n2, kb, D_MODEL), lambda i, t: (i, 0, t, 0)),
            pl.BlockSpec((None, None, N_SUBLAYERS * N_MOD, D_MODEL), lambda i, t: (layer, i, 0, 0)),
            pl.BlockSpec((2 * FNET_GROUP_DIM, FNET_GROUP_DIM), lambda i, t: (0, 0)),
            pl.BlockSpec((kb * n2, kb * n2), lambda i, t: (0, 0)),
            pl.BlockSpec((None, D_MODEL, D_MODEL), lambda i, t: (a, 0, 0)),
        ],
        out_specs=pl.BlockSpec((None, n2, kb, D_MODEL), lambda i, t: (i, 0, t, 0)),
        out_shape=jax.ShapeDtypeStruct(x.shape, F32),
        scratch_shapes=[pltpu.VMEM((2, kb * n2, D_MODEL), BF16)],
        compiler_params=_params("parallel", "parallel"),
        name=f"fnet_dft2_l{layer}",
    )(t, y, x, mod, cs, perm, w_o)


def _fourier_mix(x, mod, gain, w_o, tables, layer, a):
    b, s, _ = x.shape
    m1, t, cs, perm = tables
    n1, n2 = _dft_split(s)
    y = _dft1(x.reshape(b, n1, n2, D_MODEL), mod, gain, m1, layer)
    out = _dft2_residual(y, x.reshape(b, n2, n1, D_MODEL), mod, t, cs, perm, w_o, layer, a)
    return out.reshape(b, s, D_MODEL)


def _run_trunk(x, mod, w):
    s = x.shape[1]
    cos, sin = _rope_tables(s)
    tables = _dft_tables(s)
    for l in range(DEPTH):
        gains = w["norm_gain"][l]
        x = _ffn(x, mod, gains[0:1], w["ffn_w_in"], w["ffn_w_out"], l, 0, 0)
        a = l // 2
        if l % 2 == 0:
            q, k4, vt = _qkv(x, mod, gains[1:2], w["attn_w_qkv"], w["q_gain"][a], w["k_gain"][a],
                             cos, sin, w["ones_bd"], l, a)
            o = _attention(q, k4, vt, w["score_bound"][a], l)
            x = _ffn(x, mod, gains[2:3], w["ffn_w_in"], w["ffn_w_out"], l, 1, 2,
                     mixed=o, w_proj=w["attn_w_o"], proj_layer=a)
        else:
            x = _fourier_mix(x, mod, gains[1:2], w["fnet_w_o"], tables, l, a)
            x = _ffn(x, mod, gains[2:3], w["ffn_w_in"], w["ffn_w_out"], l, 1, 2)
    return x


def kernel(x_prompt, x_sample, c_prompt, c_sample, norm_gain, ada_w, ada_b, ffn_w_in, ffn_w_out,
           attn_w_qkv, attn_q_gain, attn_k_gain, attn_w_o, fnet_w_o):
    head = np.arange(Q_DIM) // HEAD_DIM
    w = {
        "norm_gain": norm_gain,
        "ffn_w_in": ffn_w_in.astype(BF16),
        "ffn_w_out": ffn_w_out.astype(BF16),
        "attn_w_qkv": attn_w_qkv.astype(BF16),
        "attn_w_o": attn_w_o.astype(BF16),
        "fnet_w_o": fnet_w_o.astype(BF16),
        "q_gain": jnp.tile(attn_q_gain, (1, N_HEADS))[:, None, :],
        "k_gain": jnp.tile(attn_k_gain, (1, N_KV_HEADS))[:, None, :],
        "ones_bd": jnp.asarray(head[:, None] == head[None, :], BF16),
        "score_bound": (SCORE_BOUND_SLACK * math.sqrt(HEAD_DIM) * LOG2E
                        * jnp.max(jnp.abs(attn_q_gain), axis=1) * jnp.max(jnp.abs(attn_k_gain), axis=1)),
    }
    nb = x_prompt.shape[0]
    mod = _modulation(jnp.concatenate([c_prompt, c_sample], axis=0), ada_w, ada_b)
    y_prompt = _run_trunk(x_prompt, mod[:, :nb], w)
    y_sample = _run_trunk(x_sample, mod[:, nb:], w)
    return (y_prompt, y_sample)
```

```python
import functools
import math

import numpy as np
import jax
import jax.numpy as jnp
from jax import lax
from jax.experimental import pallas as pl
from jax.experimental.pallas import tpu as pltpu

F32 = jnp.float32
BF16 = jnp.bfloat16

D_MODEL = 1024
DEPTH = 4
N_HEADS = 16
N_KV_HEADS = 4
KV_GROUP = N_HEADS // N_KV_HEADS
HEAD_DIM = D_MODEL // N_HEADS
Q_DIM = N_HEADS * HEAD_DIM
KV_DIM = N_KV_HEADS * HEAD_DIM
QKV_DIM = Q_DIM + 2 * KV_DIM
GROUP_W = KV_GROUP * HEAD_DIM
ROPE_AXIS_DIM = HEAD_DIM // 2
ROPE_THETA = 10000.0
FNET_GROUPS = 8
FNET_GROUP_DIM = D_MODEL // FNET_GROUPS
D_FF = 2816
GRID_W = 64
N_SUBLAYERS = 3
N_MOD = 3
EPS = 1e-6
LOG2E = 1.4426950408889634

LANES = 128
VMEM_LIMIT = 48 * 1024 * 1024
BIG_VMEM_LIMIT = 56 * 1024 * 1024

FFN_TM = 1024
FFN_GROUP = 512
TOK_TM = 512
QKV_SUB = 2
ATTN_TQ = 256
ATTN_BLOCKS = 16
ATTN_TK = 2048
ATTN_NOSHIFT_MAX_SCORE = 48.0
SCORE_BOUND_SLACK = 1.02
VT_ROWS = HEAD_DIM + 16
DFT_N2_MIN = 64
DFT_N1_MAX = 64
DFT_KB = 8
DFT_SB = 16


def _params(*sem, vmem_limit=VMEM_LIMIT):
    return pltpu.CompilerParams(dimension_semantics=sem, vmem_limit_bytes=vmem_limit)


def _dot(a, b):
    return jnp.dot(a, b, preferred_element_type=F32)


def _modnorm(x, gain, shift, scale):
    ms = jnp.mean(x * x, axis=-1, keepdims=True)
    return (x * lax.rsqrt(ms + EPS)) * (gain * (1.0 + scale)) + shift


def _mod_kernel(c_ref, w_ref, b_ref, o_ref):
    c = c_ref[...]
    a = (c * jax.nn.sigmoid(c)).astype(BF16)
    o_ref[...] = _dot(a, w_ref[...].astype(BF16)) + b_ref[...]


def _modulation(c, ada_w, ada_b):
    b = c.shape[0]
    n = N_SUBLAYERS * N_MOD
    out = pl.pallas_call(
        _mod_kernel,
        grid=(DEPTH, n),
        in_specs=[
            pl.BlockSpec((b, D_MODEL), lambda l, j: (0, 0)),
            pl.BlockSpec((None, D_MODEL, D_MODEL), lambda l, j: (l, 0, j)),
            pl.BlockSpec((None, 1, D_MODEL), lambda l, j: (l, 0, j)),
        ],
        out_specs=pl.BlockSpec((None, b, D_MODEL), lambda l, j: (l, 0, j)),
        out_shape=jax.ShapeDtypeStruct((DEPTH, b, n * D_MODEL), F32),
        compiler_params=_params("parallel", "parallel"),
        name="adaln_mod",
    )(c, ada_w, ada_b.reshape(DEPTH, 1, n * D_MODEL))
    return out.reshape(DEPTH, b, n, D_MODEL)


def _ffn_update(x, mod, gain, wg_ref, wu_ref, wo_ref, sub):
    h = _modnorm(x, gain, mod[3 * sub:3 * sub + 1], mod[3 * sub + 1:3 * sub + 2]).astype(BF16)
    g = _dot(h, wg_ref[...])
    u = _dot(h, wu_ref[...])
    a = (g * jax.nn.sigmoid(g) * u).astype(BF16)
    return x + (0.5 * mod[3 * sub + 2:3 * sub + 3]) * _dot(a, wo_ref[...])


def _row_groups(tm):
    group = min(FFN_GROUP, tm)
    return [slice(r, r + group) for r in range(0, tm, group)]


def _ffn_kernel(x_ref, mod_ref, gain_ref, wg_ref, wu_ref, wo_ref, o_ref, *, sub):
    for rows in _row_groups(x_ref.shape[0]):
        o_ref[rows, :] = _ffn_update(x_ref[rows, :], mod_ref[...], gain_ref[...], wg_ref, wu_ref, wo_ref, sub)


def _proj_ffn_kernel(x_ref, a_ref, wp_ref, mod_ref, gain_ref, wg_ref, wu_ref, wo_ref, o_ref, *, sub):
    mod = mod_ref[...]
    for rows in _row_groups(x_ref.shape[0]):
        x = x_ref[rows, :] + mod[3 * (sub - 1) + 2:3 * (sub - 1) + 3] * _dot(a_ref[rows, :], wp_ref[...])
        o_ref[rows, :] = _ffn_update(x, mod, gain_ref[...], wg_ref, wu_ref, wo_ref, sub)


def _resident(block_shape, index_map):
    return pl.BlockSpec(block_shape, index_map, pipeline_mode=pl.Buffered(1))


def _ffn(x, mod, gain, w_in, w_out, layer, which, sub, mixed=None, w_proj=None, proj_layer=None):
    b, s, _ = x.shape
    tm = min(FFN_TM, s)
    tok = pl.BlockSpec((None, tm, D_MODEL), lambda i, t: (i, t, 0))
    specs = [
        pl.BlockSpec((None, None, N_SUBLAYERS * N_MOD, D_MODEL), lambda i, t: (layer, i, 0, 0)),
        pl.BlockSpec((1, D_MODEL), lambda i, t: (0, 0)),
        _resident((None, None, D_MODEL, D_FF), lambda i, t: (layer, which, 0, 0)),
        _resident((None, None, D_MODEL, D_FF), lambda i, t: (layer, which, 0, 1)),
        _resident((None, None, D_FF, D_MODEL), lambda i, t: (layer, which, 0, 0)),
    ]
    args = (mod, gain, w_in, w_in, w_out)
    if mixed is None:
        body, specs, args = _ffn_kernel, [tok] + specs, (x,) + args
    else:
        proj = _resident((None, D_MODEL, D_MODEL), lambda i, t: (proj_layer, 0, 0))
        body, specs, args = _proj_ffn_kernel, [tok, tok, proj] + specs, (x, mixed, w_proj) + args
    return pl.pallas_call(
        functools.partial(body, sub=sub),
        grid=(b, s // tm),
        in_specs=specs,
        out_specs=tok,
        out_shape=jax.ShapeDtypeStruct(x.shape, F32),
        compiler_params=_params("parallel", "parallel", vmem_limit=BIG_VMEM_LIMIT),
        name=f"ffn_l{layer}_{which}",
    )(*args)


def _head_rms(t, ones_bd, gain):
    ss = _dot((t * t).astype(BF16), ones_bd)
    return t * lax.rsqrt(ss * (1.0 / HEAD_DIM) + EPS) * gain


def _rope(t, cos, sin_signed):
    n = t.shape[1]
    lane = lax.broadcasted_iota(jnp.int32, t.shape, 1)
    nxt = pltpu.roll(t, n - 1, 1)
    prv = pltpu.roll(t, 1, 1)
    partner = jnp.where(lane % 2 == 0, nxt, prv)
    reps = n // LANES
    cosf = jnp.concatenate([cos] * reps, axis=1)
    sinf = jnp.concatenate([sin_signed] * reps, axis=1)
    return t * cosf + partner * sinf


def _repeat_heads(t):
    lane = lax.broadcasted_iota(jnp.int32, (t.shape[0], LANES), 1)
    low = lane < HEAD_DIM
    cols = []
    for j in range(KV_DIM // LANES):
        c = t[:, j * LANES:(j + 1) * LANES]
        sw = pltpu.roll(c, HEAD_DIM, 1)
        first = jnp.where(low, c, sw)
        second = jnp.where(low, sw, c)
        cols += [first] * (GROUP_W // LANES) + [second] * (GROUP_W // LANES)
    return jnp.concatenate(cols, axis=1)


def _qkv_kernel(x_ref, mod_ref, gain_ref, w_ref, qg_ref, kg_ref, cos_ref, sin_ref, bd_ref,
                q_ref, k_ref, vt_ref):
    mod = mod_ref[...]
    tm = x_ref.shape[0]
    half = tm // QKV_SUB
    for u in range(QKV_SUB):
        rows = slice(u * half, (u + 1) * half)
        h = _modnorm(x_ref[rows, :], gain_ref[...], mod[3:4], mod[4:5]).astype(BF16)
        qkv = _dot(h, w_ref[...])
        q = qkv[:, :Q_DIM]
        k = qkv[:, Q_DIM:Q_DIM + KV_DIM]
        v = qkv[:, Q_DIM + KV_DIM:]
        cos = cos_ref[rows, :]
        sin = sin_ref[rows, :]
        q = _rope(_head_rms(q, bd_ref[...], qg_ref[...]), cos, sin)
        k = _rope(_head_rms(k, bd_ref[:KV_DIM, :KV_DIM], kg_ref[...]), cos, sin)
        q_ref[rows, :] = (q * (LOG2E / math.sqrt(HEAD_DIM))).astype(BF16)
        k_ref[rows, :] = _repeat_heads(k).astype(BF16)
        v_t = v.T
        ones = jnp.ones((VT_ROWS - HEAD_DIM, half), BF16)
        for j in range(N_KV_HEADS):
            vt_ref[j, :HEAD_DIM, rows] = v_t[j * HEAD_DIM:(j + 1) * HEAD_DIM].astype(BF16)
            vt_ref[j, HEAD_DIM:, rows] = ones


def _rope_tables(s):
    rows = s // GRID_W
    r = jnp.repeat(jnp.arange(rows), GRID_W).astype(F32)
    c = jnp.tile(jnp.arange(GRID_W), rows).astype(F32)
    inv = ROPE_THETA ** (-jnp.arange(0, ROPE_AXIS_DIM, 2, dtype=F32) / ROPE_AXIS_DIM)
    ang = jnp.concatenate([r[:, None] * inv, c[:, None] * inv], axis=-1)
    cos = jnp.repeat(jnp.cos(ang), 2, axis=-1)
    sin = jnp.repeat(jnp.sin(ang), 2, axis=-1)
    sign = jnp.tile(jnp.array([-1.0, 1.0], F32), HEAD_DIM // 2)
    reps = LANES // HEAD_DIM
    return jnp.tile(cos, (1, reps)), jnp.tile(sin * sign, (1, reps))


def _qkv(x, mod, gain, w_qkv, q_gain, k_gain, cos, sin, ones_bd, layer, a):
    b, s, _ = x.shape
    tm = min(TOK_TM, s)
    tok = lambda i, t: (i, t, 0)
    const2 = lambda i, t: (0, 0)
    shape = jax.ShapeDtypeStruct((b, s, D_MODEL), BF16)
    tk = min(ATTN_TK, s)
    per = tk // tm
    vt_shape = jax.ShapeDtypeStruct((b, N_KV_HEADS, s // tk, VT_ROWS, tk), BF16)
    vt_spec = pl.BlockSpec((None, N_KV_HEADS, None, VT_ROWS, tm), lambda i, t: (i, 0, t // per, 0, t % per))
    return pl.pallas_call(
        _qkv_kernel,
        grid=(b, s // tm),
        in_specs=[
            pl.BlockSpec((None, tm, D_MODEL), tok),
            pl.BlockSpec((None, None, N_SUBLAYERS * N_MOD, D_MODEL), lambda i, t: (layer, i, 0, 0)),
            pl.BlockSpec((1, D_MODEL), const2),
            pl.BlockSpec((None, D_MODEL, QKV_DIM), lambda i, t: (a, 0, 0)),
            pl.BlockSpec((1, Q_DIM), const2),
            pl.BlockSpec((1, KV_DIM), const2),
            pl.BlockSpec((tm, LANES), lambda i, t: (t, 0)),
            pl.BlockSpec((tm, LANES), lambda i, t: (t, 0)),
            pl.BlockSpec((Q_DIM, Q_DIM), const2),
        ],
        out_specs=[pl.BlockSpec((None, tm, D_MODEL), tok)] * 2 + [vt_spec],
        out_shape=[shape, shape, vt_shape],
        compiler_params=_params("parallel", "parallel"),
        name=f"qkv_l{layer}",
    )(x, mod, gain, w_qkv, q_gain, k_gain, cos, sin, ones_bd)


def _attn_kernel(q_ref, k_ref, vt_ref, o_ref, *, tq, nsub, tk, nk, shift):
    head = lax.broadcasted_iota(jnp.int32, (tq, GROUP_W), 1) // HEAD_DIM
    w = KV_GROUP * tq
    qs = []
    for u in range(nsub):
        q = q_ref[u * tq:(u + 1) * tq, :]
        zero = jnp.zeros_like(q)
        qs.append(jnp.concatenate([jnp.where(head == g, q, zero) for g in range(KV_GROUP)], axis=0))

    def scores(c, u):
        kc = k_ref[pl.ds(pl.multiple_of(c * tk, tk), tk), :]
        return lax.dot_general(kc, qs[u], (((1,), (1,)), ((), ())), preferred_element_type=F32)

    def chunk_shifted(c, carry):
        out = []
        for u, (m, acc) in enumerate(carry):
            st = scores(c, u)
            m_new = jnp.maximum(m, jnp.max(st, axis=0, keepdims=True))
            p = jnp.exp2(st - m_new).astype(BF16)
            out.append((m_new, jnp.exp2(m - m_new) * acc + _dot(vt_ref[c], p)))
        return tuple(out)

    def chunk_plain(c, accs):
        return tuple(acc + _dot(vt_ref[c], jnp.exp2(scores(c, u)).astype(BF16)) for u, acc in enumerate(accs))

    acc0 = jnp.zeros((VT_ROWS, w), F32)
    if shift:
        carry = lax.fori_loop(0, nk, chunk_shifted, ((jnp.full((1, w), -1e30, F32), acc0),) * nsub)
        accs = [acc for _, acc in carry]
    else:
        accs = lax.fori_loop(0, nk, chunk_plain, (acc0,) * nsub, unroll=True)
    for u, acc in enumerate(accs):
        out = acc[:HEAD_DIM] / acc[HEAD_DIM:HEAD_DIM + 1]
        o_t = jnp.concatenate([out[:, g * tq:(g + 1) * tq] for g in range(KV_GROUP)], axis=0)
        o_ref[u * tq:(u + 1) * tq, :] = o_t.T.astype(BF16)


def _attention(q, k4, vt, score_bound, layer):
    b, s, _ = q.shape
    nk, tk = vt.shape[2], vt.shape[4]
    tq = min(ATTN_TQ, s)
    nsub = min(ATTN_BLOCKS // nk, s // tq)
    rows = nsub * tq

    def call(shift):
        return pl.pallas_call(
            functools.partial(_attn_kernel, tq=tq, nsub=nsub, tk=tk, nk=nk, shift=shift),
            grid=(b, N_KV_HEADS, s // rows),
            in_specs=[
                pl.BlockSpec((None, rows, GROUP_W), lambda i, j, t: (i, t, j)),
                pl.BlockSpec((None, s, GROUP_W), lambda i, j, t: (i, 0, j)),
                pl.BlockSpec((None, None, nk, VT_ROWS, tk), lambda i, j, t: (i, j, 0, 0, 0)),
            ],
            out_specs=pl.BlockSpec((None, rows, GROUP_W), lambda i, j, t: (i, t, j)),
            out_shape=jax.ShapeDtypeStruct((b, s, D_MODEL), BF16),
            compiler_params=_params("parallel", "parallel", "parallel", vmem_limit=BIG_VMEM_LIMIT),
            name=f"attn_l{layer}_{'shifted' if shift else 'plain'}",
        )

    return lax.cond(score_bound <= ATTN_NOSHIFT_MAX_SCORE,
                    lambda *a: call(False)(*a), lambda *a: call(True)(*a), q, k4, vt)


def _dft_split(s):
    n2 = max(DFT_N2_MIN, s // DFT_N1_MAX)
    return s // n2, n2


def _dft_tables(s):
    n1, n2 = _dft_split(s)
    k1 = np.arange(n1)
    ang1 = 2.0 * np.pi * ((k1[:, None] * k1[None, :]) % n1) / n1
    m1 = np.concatenate([np.cos(ang1), -np.sin(ang1)], axis=0)
    m1 = np.kron(m1, np.eye(DFT_SB))
    src = (np.arange(DFT_KB)[None, :] * n2 + np.arange(n2)[:, None]).reshape(-1)
    perm = np.zeros((DFT_KB * n2, DFT_KB * n2))
    perm[np.arange(DFT_KB * n2), src] = 1.0
    s2 = np.arange(n2)
    num = (s2[None, None, :] * k1[:, None, None] + s2[None, None, :] * s2[None, :, None] * n1) % s
    psi = 2.0 * np.pi * num / s
    tr, ti = np.cos(psi), -np.sin(psi)
    t = np.concatenate([np.concatenate([tr, -ti], axis=2), np.concatenate([ti, tr], axis=2)], axis=1)
    c = np.arange(FNET_GROUP_DIM)
    phi = 2.0 * np.pi * ((c[:, None] * c[None, :]) % FNET_GROUP_DIM) / FNET_GROUP_DIM
    cs = np.concatenate([np.cos(phi), np.sin(phi)], axis=0)
    return jnp.asarray(m1, BF16), jnp.asarray(t, BF16), jnp.asarray(cs, BF16), jnp.asarray(perm, BF16)


def _dft1_kernel(x_ref, mod_ref, gain_ref, m_ref, y_ref, *, n1, sb):
    mod = mod_ref[...]
    x = x_ref[...].reshape(n1 * sb, D_MODEL)
    h = _modnorm(x, gain_ref[...], mod[3:4], mod[4:5]).astype(BF16)
    y_ref[...] = _dot(m_ref[...], h).astype(BF16).reshape(2, n1, sb, D_MODEL)


def _dft1(x, mod, gain, m1, layer):
    b, n1, n2, _ = x.shape
    sb = DFT_SB
    return pl.pallas_call(
        functools.partial(_dft1_kernel, n1=n1, sb=sb),
        grid=(b, n2 // sb),
        in_specs=[
            pl.BlockSpec((None, n1, sb, D_MODEL), lambda i, t: (i, 0, t, 0)),
            pl.BlockSpec((None, None, N_SUBLAYERS * N_MOD, D_MODEL), lambda i, t: (layer, i, 0, 0)),
            pl.BlockSpec((1, D_MODEL), lambda i, t: (0, 0)),
            _resident((2 * n1 * sb, n1 * sb), lambda i, t: (0, 0)),
        ],
        out_specs=pl.BlockSpec((None, 2, n1, sb, D_MODEL), lambda i, t: (i, 0, 0, t, 0)),
        out_shape=jax.ShapeDtypeStruct((b, 2, n1, n2, D_MODEL), BF16),
        compiler_params=_params("parallel", "parallel"),
        name=f"fnet_dft1_l{layer}",
    )(x, mod, gain, m1)


def _dft2_kernel(t_ref, y_ref, x_ref, mod_ref, cs_ref, perm_ref, w_ref, out_ref, u_ref, *, kb, n2, inv_norm):
    for i in range(kb):
        y = jnp.concatenate([y_ref[0, i], y_ref[1, i]], axis=0)
        u = _dot(t_ref[i], y).astype(BF16)
        u_ref[0, i * n2:(i + 1) * n2, :] = u[:n2]
        u_ref[1, i * n2:(i + 1) * n2, :] = u[n2:]
    ur = u_ref[0]
    ui = u_ref[1]
    cs = cs_ref[...]
    parts = []
    for g in range(FNET_GROUPS):
        cols = slice(g * FNET_GROUP_DIM, (g + 1) * FNET_GROUP_DIM)
        parts.append(_dot(jnp.concatenate([ur[:, cols], ui[:, cols]], axis=1), cs))
    f = (jnp.concatenate(parts, axis=1) * inv_norm).astype(BF16)
    f = _dot(perm_ref[...], f).astype(BF16)
    r = mod_ref[...][5:6] * _dot(f, w_ref[...])
    out_ref[...] = x_ref[...] + r.reshape(n2, kb, D_MODEL)


def _dft2_residual(y, x, mod, t, cs, perm, w_o, layer, a):
    b, _, n1, n2, _ = y.shape
    kb = DFT_KB
    inv_norm = 1.0 / math.sqrt(n1 * n2 * FNET_GROUP_DIM)
    return pl.pallas_call(
        functools.partial(_dft2_kernel, kb=kb, n2=n2, inv_norm=inv_norm),
        grid=(b, n1 // kb),
        in_specs=[
            pl.BlockSpec((kb, 2 * n2, 2 * n2), lambda i, t: (t, 0, 0)),
            pl.BlockSpec((None, 2, kb, n2, D_MODEL), lambda i, t: (i, 0, t, 0, 0)),
            pl.BlockSpec((None, n2, kb, D_MODEL), lambda i, t: (i, 0, t, 0)),
            pl.BlockSpec((None, None, N_SUBLAYERS * N_MOD, D_MODEL), lambda i, t: (layer, i, 0, 0)),
            pl.BlockSpec((2 * FNET_GROUP_DIM, FNET_GROUP_DIM), lambda i, t: (0, 0)),
            pl.BlockSpec((kb * n2, kb * n2), lambda i, t: (0, 0)),
            pl.BlockSpec((None, D_MODEL, D_MODEL), lambda i, t: (a, 0, 0)),
        ],
        out_specs=pl.BlockSpec((None, n2, kb, D_MODEL), lambda i, t: (i, 0, t, 0)),
        out_shape=jax.ShapeDtypeStruct(x.shape, F32),
        scratch_shapes=[pltpu.VMEM((2, kb * n2, D_MODEL), BF16)],
        compiler_params=_params("parallel", "parallel"),
        name=f"fnet_dft2_l{layer}",
    )(t, y, x, mod, cs, perm, w_o)


def _fourier_mix(x, mod, gain, w_o, tables, layer, a):
    b, s, _ = x.shape
    m1, t, cs, perm = tables
    n1, n2 = _dft_split(s)
    y = _dft1(x.reshape(b, n1, n2, D_MODEL), mod, gain, m1, layer)
    out = _dft2_residual(y, x.reshape(b, n2, n1, D_MODEL), mod, t, cs, perm, w_o, layer, a)
    return out.reshape(b, s, D_MODEL)


def _run_trunk(x, mod, w):
    s = x.shape[1]
    cos, sin = _rope_tables(s)
    tables = _dft_tables(s)
    for l in range(DEPTH):
        gains = w["norm_gain"][l]
        x = _ffn(x, mod, gains[0:1], w["ffn_w_in"], w["ffn_w_out"], l, 0, 0)
        a = l // 2
        if l % 2 == 0:
            q, k4, vt = _qkv(x, mod, gains[1:2], w["attn_w_qkv"], w["q_gain"][a], w["k_gain"][a],
                             cos, sin, w["ones_bd"], l, a)
            o = _attention(q, k4, vt, w["score_bound"][a], l)
            x = _ffn(x, mod, gains[2:3], w["ffn_w_in"], w["ffn_w_out"], l, 1, 2,
                     mixed=o, w_proj=w["attn_w_o"], proj_layer=a)
        else:
            x = _fourier_mix(x, mod, gains[1:2], w["fnet_w_o"], tables, l, a)
            x = _ffn(x, mod, gains[2:3], w["ffn_w_in"], w["ffn_w_out"], l, 1, 2)
    return x


def kernel(x_prompt, x_sample, c_prompt, c_sample, norm_gain, ada_w, ada_b, ffn_w_in, ffn_w_out,
           attn_w_qkv, attn_q_gain, attn_k_gain, attn_w_o, fnet_w_o):
    head = np.arange(Q_DIM) // HEAD_DIM
    w = {
        "norm_gain": norm_gain,
        "ffn_w_in": ffn_w_in.astype(BF16),
        "ffn_w_out": ffn_w_out.astype(BF16),
        "attn_w_qkv": attn_w_qkv.astype(BF16),
        "attn_w_o": attn_w_o.astype(BF16),
        "fnet_w_o": fnet_w_o.astype(BF16),
        "q_gain": jnp.tile(attn_q_gain, (1, N_HEADS))[:, None, :],
        "k_gain": jnp.tile(attn_k_gain, (1, N_KV_HEADS))[:, None, :],
        "ones_bd": jnp.asarray(head[:, None] == head[None, :], BF16),
        "score_bound": (SCORE_BOUND_SLACK * math.sqrt(HEAD_DIM) * LOG2E
                        * jnp.max(jnp.abs(attn_q_gain), axis=1) * jnp.max(jnp.abs(attn_k_gain), axis=1)),
    }
    nb = x_prompt.shape[0]
    mod = _modulation(jnp.concatenate([c_prompt, c_sample], axis=0), ada_w, ada_b)
    y_prompt = _run_trunk(x_prompt, mod[:, :nb], w)
    y_sample = _run_trunk(x_sample, mod[:, nb:], w)
    return (y_prompt, y_sample)
```

```python
import functools
import math

import numpy as np
import jax
import jax.numpy as jnp
from jax import lax
from jax.experimental import pallas as pl
from jax.experimental.pallas import tpu as pltpu

F32 = jnp.float32
BF16 = jnp.bfloat16

D_MODEL = 1024
DEPTH = 4
N_HEADS = 16
N_KV_HEADS = 4
KV_GROUP = N_HEADS // N_KV_HEADS
HEAD_DIM = D_MODEL // N_HEADS
Q_DIM = N_HEADS * HEAD_DIM
KV_DIM = N_KV_HEADS * HEAD_DIM
QKV_DIM = Q_DIM + 2 * KV_DIM
GROUP_W = KV_GROUP * HEAD_DIM
ROPE_AXIS_DIM = HEAD_DIM // 2
ROPE_THETA = 10000.0
FNET_GROUPS = 8
FNET_GROUP_DIM = D_MODEL // FNET_GROUPS
D_FF = 2816
GRID_W = 64
N_SUBLAYERS = 3
N_MOD = 3
EPS = 1e-6
LOG2E = 1.4426950408889634

LANES = 128
VMEM_LIMIT = 48 * 1024 * 1024
BIG_VMEM_LIMIT = 56 * 1024 * 1024

FFN_TM = 1024
FFN_GROUP = 512
TOK_TM = 1024
QKV_SUB = 4
ATTN_TQ = 256
ATTN_BLOCKS = 16
ATTN_TK = 2048
ATTN_NOSHIFT_MAX_SCORE = 48.0
SCORE_BOUND_SLACK = 1.02
VT_ROWS = HEAD_DIM + 16
DFT_N2_MIN = 64
DFT_N1_MAX = 64
DFT_KB = 8
DFT_SB = 16


def _params(*sem, vmem_limit=VMEM_LIMIT):
    return pltpu.CompilerParams(dimension_semantics=sem, vmem_limit_bytes=vmem_limit)


def _dot(a, b):
    return jnp.dot(a, b, preferred_element_type=F32)


def _modnorm(x, gain, shift, scale):
    ms = jnp.mean(x * x, axis=-1, keepdims=True)
    return (x * lax.rsqrt(ms + EPS)) * (gain * (1.0 + scale)) + shift


def _mod_kernel(c_ref, w_ref, b_ref, o_ref):
    c = c_ref[...]
    a = (c * jax.nn.sigmoid(c)).astype(BF16)
    o_ref[...] = _dot(a, w_ref[...].astype(BF16)) + b_ref[...]


def _modulation(c, ada_w, ada_b):
    b = c.shape[0]
    n = N_SUBLAYERS * N_MOD
    out = pl.pallas_call(
        _mod_kernel,
        grid=(DEPTH, n),
        in_specs=[
            pl.BlockSpec((b, D_MODEL), lambda l, j: (0, 0)),
            pl.BlockSpec((None, D_MODEL, D_MODEL), lambda l, j: (l, 0, j)),
            pl.BlockSpec((None, 1, D_MODEL), lambda l, j: (l, 0, j)),
        ],
        out_specs=pl.BlockSpec((None, b, D_MODEL), lambda l, j: (l, 0, j)),
        out_shape=jax.ShapeDtypeStruct((DEPTH, b, n * D_MODEL), F32),
        compiler_params=_params("parallel", "parallel"),
        name="adaln_mod",
    )(c, ada_w, ada_b.reshape(DEPTH, 1, n * D_MODEL))
    return out.reshape(DEPTH, b, n, D_MODEL)


def _ffn_update(x, mod, gain, wg_ref, wu_ref, wo_ref, sub):
    h = _modnorm(x, gain, mod[3 * sub:3 * sub + 1], mod[3 * sub + 1:3 * sub + 2]).astype(BF16)
    g = _dot(h, wg_ref[...])
    u = _dot(h, wu_ref[...])
    a = (g * jax.nn.sigmoid(g) * u).astype(BF16)
    return x + (0.5 * mod[3 * sub + 2:3 * sub + 3]) * _dot(a, wo_ref[...])


def _row_groups(tm):
    group = min(FFN_GROUP, tm)
    return [slice(r, r + group) for r in range(0, tm, group)]


def _ffn_kernel(x_ref, mod_ref, gain_ref, wg_ref, wu_ref, wo_ref, o_ref, *, sub):
    for rows in _row_groups(x_ref.shape[0]):
        o_ref[rows, :] = _ffn_update(x_ref[rows, :], mod_ref[...], gain_ref[...], wg_ref, wu_ref, wo_ref, sub)


def _proj_ffn_kernel(x_ref, a_ref, wp_ref, mod_ref, gain_ref, wg_ref, wu_ref, wo_ref, o_ref, *, sub):
    mod = mod_ref[...]
    for rows in _row_groups(x_ref.shape[0]):
        x = x_ref[rows, :] + mod[3 * (sub - 1) + 2:3 * (sub - 1) + 3] * _dot(a_ref[rows, :], wp_ref[...])
        o_ref[rows, :] = _ffn_update(x, mod, gain_ref[...], wg_ref, wu_ref, wo_ref, sub)


def _resident(block_shape, index_map):
    return pl.BlockSpec(block_shape, index_map, pipeline_mode=pl.Buffered(1))


def _ffn(x, mod, gain, w_in, w_out, layer, which, sub, mixed=None, w_proj=None, proj_layer=None):
    b, s, _ = x.shape
    tm = min(FFN_TM, s)
    tok = pl.BlockSpec((None, tm, D_MODEL), lambda i, t: (i, t, 0))
    specs = [
        pl.BlockSpec((None, None, N_SUBLAYERS * N_MOD, D_MODEL), lambda i, t: (layer, i, 0, 0)),
        pl.BlockSpec((1, D_MODEL), lambda i, t: (0, 0)),
        _resident((None, None, D_MODEL, D_FF), lambda i, t: (layer, which, 0, 0)),
        _resident((None, None, D_MODEL, D_FF), lambda i, t: (layer, which, 0, 1)),
        _resident((None, None, D_FF, D_MODEL), lambda i, t: (layer, which, 0, 0)),
    ]
    args = (mod, gain, w_in, w_in, w_out)
    if mixed is None:
        body, specs, args = _ffn_kernel, [tok] + specs, (x,) + args
    else:
        proj = _resident((None, D_MODEL, D_MODEL), lambda i, t: (proj_layer, 0, 0))
        body, specs, args = _proj_ffn_kernel, [tok, tok, proj] + specs, (x, mixed, w_proj) + args
    return pl.pallas_call(
        functools.partial(body, sub=sub),
        grid=(b, s // tm),
        in_specs=specs,
        out_specs=tok,
        out_shape=jax.ShapeDtypeStruct(x.shape, F32),
        compiler_params=_params("parallel", "parallel", vmem_limit=BIG_VMEM_LIMIT),
        name=f"ffn_l{layer}_{which}",
    )(*args)


def _head_rms(t, ones_bd, gain):
    ss = _dot((t * t).astype(BF16), ones_bd)
    return t * lax.rsqrt(ss * (1.0 / HEAD_DIM) + EPS) * gain


def _rope(t, cos, sin_signed):
    n = t.shape[1]
    lane = lax.broadcasted_iota(jnp.int32, t.shape, 1)
    nxt = pltpu.roll(t, n - 1, 1)
    prv = pltpu.roll(t, 1, 1)
    partner = jnp.where(lane % 2 == 0, nxt, prv)
    reps = n // LANES
    cosf = jnp.concatenate([cos] * reps, axis=1)
    sinf = jnp.concatenate([sin_signed] * reps, axis=1)
    return t * cosf + partner * sinf


def _repeat_heads(t):
    lane = lax.broadcasted_iota(jnp.int32, (t.shape[0], LANES), 1)
    low = lane < HEAD_DIM
    cols = []
    for j in range(KV_DIM // LANES):
        c = t[:, j * LANES:(j + 1) * LANES]
        sw = pltpu.roll(c, HEAD_DIM, 1)
        first = jnp.where(low, c, sw)
        second = jnp.where(low, sw, c)
        cols += [first] * (GROUP_W // LANES) + [second] * (GROUP_W // LANES)
    return jnp.concatenate(cols, axis=1)


def _qkv_kernel(x_ref, mod_ref, gain_ref, w_ref, qg_ref, kg_ref, cos_ref, sin_ref, bd_ref,
                q_ref, k_ref, vt_ref):
    mod = mod_ref[...]
    tm = x_ref.shape[0]
    half = tm // QKV_SUB
    for u in range(QKV_SUB):
        rows = slice(u * half, (u + 1) * half)
        h = _modnorm(x_ref[rows, :], gain_ref[...], mod[3:4], mod[4:5]).astype(BF16)
        qkv = _dot(h, w_ref[...])
        q = qkv[:, :Q_DIM]
        k = qkv[:, Q_DIM:Q_DIM + KV_DIM]
        v = qkv[:, Q_DIM + KV_DIM:]
        cos = cos_ref[rows, :]
        sin = sin_ref[rows, :]
        q = _rope(_head_rms(q, bd_ref[...], qg_ref[...]), cos, sin)
        k = _rope(_head_rms(k, bd_ref[:KV_DIM, :KV_DIM], kg_ref[...]), cos, sin)
        q_ref[rows, :] = (q * (LOG2E / math.sqrt(HEAD_DIM))).astype(BF16)
        k_ref[rows, :] = _repeat_heads(k).astype(BF16)
        v_t = v.T
        ones = jnp.ones((VT_ROWS - HEAD_DIM, half), BF16)
        for j in range(N_KV_HEADS):
            vt_ref[j, :HEAD_DIM, rows] = v_t[j * HEAD_DIM:(j + 1) * HEAD_DIM].astype(BF16)
            vt_ref[j, HEAD_DIM:, rows] = ones


def _rope_tables(s):
    rows = s // GRID_W
    r = jnp.repeat(jnp.arange(rows), GRID_W).astype(F32)
    c = jnp.tile(jnp.arange(GRID_W), rows).astype(F32)
    inv = ROPE_THETA ** (-jnp.arange(0, ROPE_AXIS_DIM, 2, dtype=F32) / ROPE_AXIS_DIM)
    ang = jnp.concatenate([r[:, None] * inv, c[:, None] * inv], axis=-1)
    cos = jnp.repeat(jnp.cos(ang), 2, axis=-1)
    sin = jnp.repeat(jnp.sin(ang), 2, axis=-1)
    sign = jnp.tile(jnp.array([-1.0, 1.0], F32), HEAD_DIM // 2)
    reps = LANES // HEAD_DIM
    return jnp.tile(cos, (1, reps)), jnp.tile(sin * sign, (1, reps))


def _qkv(x, mod, gain, w_qkv, q_gain, k_gain, cos, sin, ones_bd, layer, a):
    b, s, _ = x.shape
    tm = min(TOK_TM, s)
    tok = lambda i, t: (i, t, 0)
    const2 = lambda i, t: (0, 0)
    shape = jax.ShapeDtypeStruct((b, s, D_MODEL), BF16)
    tk = min(ATTN_TK, s)
    per = tk // tm
    vt_shape = jax.ShapeDtypeStruct((b, N_KV_HEADS, s // tk, VT_ROWS, tk), BF16)
    vt_spec = pl.BlockSpec((None, N_KV_HEADS, None, VT_ROWS, tm), lambda i, t: (i, 0, t // per, 0, t % per))
    return pl.pallas_call(
        _qkv_kernel,
        grid=(b, s // tm),
        in_specs=[
            pl.BlockSpec((None, tm, D_MODEL), tok),
            pl.BlockSpec((None, None, N_SUBLAYERS * N_MOD, D_MODEL), lambda i, t: (layer, i, 0, 0)),
            pl.BlockSpec((1, D_MODEL), const2),
            pl.BlockSpec((None, D_MODEL, QKV_DIM), lambda i, t: (a, 0, 0)),
            pl.BlockSpec((1, Q_DIM), const2),
            pl.BlockSpec((1, KV_DIM), const2),
            pl.BlockSpec((tm, LANES), lambda i, t: (t, 0)),
            pl.BlockSpec((tm, LANES), lambda i, t: (t, 0)),
            pl.BlockSpec((Q_DIM, Q_DIM), const2),
        ],
        out_specs=[pl.BlockSpec((None, tm, D_MODEL), tok)] * 2 + [vt_spec],
        out_shape=[shape, shape, vt_shape],
        compiler_params=_params("parallel", "parallel"),
        name=f"qkv_l{layer}",
    )(x, mod, gain, w_qkv, q_gain, k_gain, cos, sin, ones_bd)


def _attn_kernel(q_ref, k_ref, vt_ref, o_ref, *, tq, nsub, tk, nk, shift):
    head = lax.broadcasted_iota(jnp.int32, (tq, GROUP_W), 1) // HEAD_DIM
    w = KV_GROUP * tq
    qs = []
    for u in range(nsub):
        q = q_ref[u * tq:(u + 1) * tq, :]
        zero = jnp.zeros_like(q)
        qs.append(jnp.concatenate([jnp.where(head == g, q, zero) for g in range(KV_GROUP)], axis=0))

    def scores(c, u):
        kc = k_ref[pl.ds(pl.multiple_of(c * tk, tk), tk), :]
        return lax.dot_general(kc, qs[u], (((1,), (1,)), ((), ())), preferred_element_type=F32)

    def chunk_shifted(c, carry):
        out = []
        for u, (m, acc) in enumerate(carry):
            st = scores(c, u)
            m_new = jnp.maximum(m, jnp.max(st, axis=0, keepdims=True))
            p = jnp.exp2(st - m_new).astype(BF16)
            out.append((m_new, jnp.exp2(m - m_new) * acc + _dot(vt_ref[c], p)))
        return tuple(out)

    def chunk_plain(c, accs):
        return tuple(acc + _dot(vt_ref[c], jnp.exp2(scores(c, u)).astype(BF16)) for u, acc in enumerate(accs))

    acc0 = jnp.zeros((VT_ROWS, w), F32)
    if shift:
        carry = lax.fori_loop(0, nk, chunk_shifted, ((jnp.full((1, w), -1e30, F32), acc0),) * nsub)
        accs = [acc for _, acc in carry]
    else:
        accs = lax.fori_loop(0, nk, chunk_plain, (acc0,) * nsub, unroll=True)
    for u, acc in enumerate(accs):
        out = acc[:HEAD_DIM] / acc[HEAD_DIM:HEAD_DIM + 1]
        o_t = jnp.concatenate([out[:, g * tq:(g + 1) * tq] for g in range(KV_GROUP)], axis=0)
        o_ref[u * tq:(u + 1) * tq, :] = o_t.T.astype(BF16)


def _attention(q, k4, vt, score_bound, layer):
    b, s, _ = q.shape
    nk, tk = vt.shape[2], vt.shape[4]
    tq = min(ATTN_TQ, s)
    nsub = min(ATTN_BLOCKS // nk, s // tq)
    rows = nsub * tq

    def call(shift):
        return pl.pallas_call(
            functools.partial(_attn_kernel, tq=tq, nsub=nsub, tk=tk, nk=nk, shift=shift),
            grid=(b, N_KV_HEADS, s // rows),
            in_specs=[
                pl.BlockSpec((None, rows, GROUP_W), lambda i, j, t: (i, t, j)),
                pl.BlockSpec((None, s, GROUP_W), lambda i, j, t: (i, 0, j)),
                pl.BlockSpec((None, None, nk, VT_ROWS, tk), lambda i, j, t: (i, j, 0, 0, 0)),
            ],
            out_specs=pl.BlockSpec((None, rows, GROUP_W), lambda i, j, t: (i, t, j)),
            out_shape=jax.ShapeDtypeStruct((b, s, D_MODEL), BF16),
            compiler_params=_params("parallel", "parallel", "parallel", vmem_limit=BIG_VMEM_LIMIT),
            name=f"attn_l{layer}_{'shifted' if shift else 'plain'}",
        )

    return lax.cond(score_bound <= ATTN_NOSHIFT_MAX_SCORE,
                    lambda *a: call(False)(*a), lambda *a: call(True)(*a), q, k4, vt)


def _dft_split(s):
    n2 = max(DFT_N2_MIN, s // DFT_N1_MAX)
    return s // n2, n2


def _dft_tables(s):
    n1, n2 = _dft_split(s)
    k1 = np.arange(n1)
    ang1 = 2.0 * np.pi * ((k1[:, None] * k1[None, :]) % n1) / n1
    m1 = np.concatenate([np.cos(ang1), -np.sin(ang1)], axis=0)
    m1 = np.kron(m1, np.eye(DFT_SB))
    src = (np.arange(DFT_KB)[None, :] * n2 + np.arange(n2)[:, None]).reshape(-1)
    perm = np.zeros((DFT_KB * n2, DFT_KB * n2))
    perm[np.arange(DFT_KB * n2), src] = 1.0
    s2 = np.arange(n2)
    num = (s2[None, None, :] * k1[:, None, None] + s2[None, None, :] * s2[None, :, None] * n1) % s
    psi = 2.0 * np.pi * num / s
    tr, ti = np.cos(psi), -np.sin(psi)
    t = np.concatenate([np.concatenate([tr, -ti], axis=2), np.concatenate([ti, tr], axis=2)], axis=1)
    c = np.arange(FNET_GROUP_DIM)
    phi = 2.0 * np.pi * ((c[:, None] * c[None, :]) % FNET_GROUP_DIM) / FNET_GROUP_DIM
    cs = np.concatenate([np.cos(phi), np.sin(phi)], axis=0)
    return jnp.asarray(m1, BF16), jnp.asarray(t, BF16), jnp.asarray(cs, BF16), jnp.asarray(perm, BF16)


def _dft1_kernel(x_ref, mod_ref, gain_ref, m_ref, y_ref, *, n1, sb):
    mod = mod_ref[...]
    x = x_ref[...].reshape(n1 * sb, D_MODEL)
    h = _modnorm(x, gain_ref[...], mod[3:4], mod[4:5]).astype(BF16)
    y_ref[...] = _dot(m_ref[...], h).astype(BF16).reshape(2, n1, sb, D_MODEL)


def _dft1(x, mod, gain, m1, layer):
    b, n1, n2, _ = x.shape
    sb = DFT_SB
    return pl.pallas_call(
        functools.partial(_dft1_kernel, n1=n1, sb=sb),
        grid=(b, n2 // sb),
        in_specs=[
            pl.BlockSpec((None, n1, sb, D_MODEL), lambda i, t: (i, 0, t, 0)),
            pl.BlockSpec((None, None, N_SUBLAYERS * N_MOD, D_MODEL), lambda i, t: (layer, i, 0, 0)),
            pl.BlockSpec((1, D_MODEL), lambda i, t: (0, 0)),
            _resident((2 * n1 * sb, n1 * sb), lambda i, t: (0, 0)),
        ],
        out_specs=pl.BlockSpec((None, 2, n1, sb, D_MODEL), lambda i, t: (i, 0, 0, t, 0)),
        out_shape=jax.ShapeDtypeStruct((b, 2, n1, n2, D_MODEL), BF16),
        compiler_params=_params("parallel", "parallel"),
        name=f"fnet_dft1_l{layer}",
    )(x, mod, gain, m1)


def _dft2_kernel(t_ref, y_ref, x_ref, mod_ref, cs_ref, perm_ref, w_ref, out_ref, u_ref, *, kb, n2, inv_norm):
    for i in range(kb):
        y = jnp.concatenate([y_ref[0, i], y_ref[1, i]], axis=0)
        u = _dot(t_ref[i], y).astype(BF16)
        u_ref[0, i * n2:(i + 1) * n2, :] = u[:n2]
        u_ref[1, i * n2:(i + 1) * n2, :] = u[n2:]
    ur = u_ref[0]
    ui = u_ref[1]
    cs = cs_ref[...]
    parts = []
    for g in range(FNET_GROUPS):
        cols = slice(g * FNET_GROUP_DIM, (g + 1) * FNET_GROUP_DIM)
        parts.append(_dot(jnp.concatenate([ur[:, cols], ui[:, cols]], axis=1), cs))
    f = (jnp.concatenate(parts, axis=1) * inv_norm).astype(BF16)
    f = _dot(perm_ref[...], f).astype(BF16)
    r = mod_ref[...][5:6] * _dot(f, w_ref[...])
    out_ref[...] = x_ref[...] + r.reshape(n2, kb, D_MODEL)


def _dft2_residual(y, x, mod, t, cs, perm, w_o, layer, a):
    b, _, n1, n2, _ = y.shape
    kb = DFT_KB
    inv_norm = 1.0 / math.sqrt(n1 * n2 * FNET_GROUP_DIM)
    return pl.pallas_call(
        functools.partial(_dft2_kernel, kb=kb, n2=n2, inv_norm=inv_norm),
        grid=(b, n1 // kb),
        in_specs=[
            pl.BlockSpec((kb, 2 * n2, 2 * n2), lambda i, t: (t, 0, 0)),
            pl.BlockSpec((None, 2, kb, n2, D_MODEL), lambda i, t: (i, 0, t, 0, 0)),
            pl.BlockSpec((None, n2, kb, D_MODEL), lambda i, t: (i, 0, t, 0)),
            pl.BlockSpec((None, None, N_SUBLAYERS * N_MOD, D_MODEL), lambda i, t: (layer, i, 0, 0)),
            pl.BlockSpec((2 * FNET_GROUP_DIM, FNET_GROUP_DIM), lambda i, t: (0, 0)),
            pl.BlockSpec((kb * n2, kb * n2), lambda i, t: (0, 0)),
            pl.BlockSpec((None, D_MODEL, D_MODEL), lambda i, t: (a, 0, 0)),
        ],
        out_specs=pl.BlockSpec((None, n2, kb, D_MODEL), lambda i, t: (i, 0, t, 0)),
        out_shape=jax.ShapeDtypeStruct(x.shape, F32),
        scratch_shapes=[pltpu.VMEM((2, kb * n2, D_MODEL), BF16)],
        compiler_params=_params("parallel", "parallel"),
        name=f"fnet_dft2_l{layer}",
    )(t, y, x, mod, cs, perm, w_o)


def _fourier_mix(x, mod, gain, w_o, tables, layer, a):
    b, s, _ = x.shape
    m1, t, cs, perm = tables
    n1, n2 = _dft_split(s)
    y = _dft1(x.reshape(b, n1, n2, D_MODEL), mod, gain, m1, layer)
    out = _dft2_residual(y, x.reshape(b, n2, n1, D_MODEL), mod, t, cs, perm, w_o, layer, a)
    return out.reshape(b, s, D_MODEL)


def _run_trunk(x, mod, w):
    s = x.shape[1]
    cos, sin = _rope_tables(s)
    tables = _dft_tables(s)
    for l in range(DEPTH):
        gains = w["norm_gain"][l]
        x = _ffn(x, mod, gains[0:1], w["ffn_w_in"], w["ffn_w_out"], l, 0, 0)
        a = l // 2
        if l % 2 == 0:
            q, k4, vt = _qkv(x, mod, gains[1:2], w["attn_w_qkv"], w["q_gain"][a], w["k_gain"][a],
                             cos, sin, w["ones_bd"], l, a)
            o = _attention(q, k4, vt, w["score_bound"][a], l)
            x = _ffn(x, mod, gains[2:3], w["ffn_w_in"], w["ffn_w_out"], l, 1, 2,
                     mixed=o, w_proj=w["attn_w_o"], proj_layer=a)
        else:
            x = _fourier_mix(x, mod, gains[1:2], w["fnet_w_o"], tables, l, a)
            x = _ffn(x, mod, gains[2:3], w["ffn_w_in"], w["ffn_w_out"], l, 1, 2)
    return x


def kernel(x_prompt, x_sample, c_prompt, c_sample, norm_gain, ada_w, ada_b, ffn_w_in, ffn_w_out,
           attn_w_qkv, attn_q_gain, attn_k_gain, attn_w_o, fnet_w_o):
    head = np.arange(Q_DIM) // HEAD_DIM
    w = {
        "norm_gain": norm_gain,
        "ffn_w_in": ffn_w_in.astype(BF16),
        "ffn_w_out": ffn_w_out.astype(BF16),
        "attn_w_qkv": attn_w_qkv.astype(BF16),
        "attn_w_o": attn_w_o.astype(BF16),
        "fnet_w_o": fnet_w_o.astype(BF16),
        "q_gain": jnp.tile(attn_q_gain, (1, N_HEADS))[:, None, :],
        "k_gain": jnp.tile(attn_k_gain, (1, N_KV_HEADS))[:, None, :],
        "ones_bd": jnp.asarray(head[:, None] == head[None, :], BF16),
        "score_bound": (SCORE_BOUND_SLACK * math.sqrt(HEAD_DIM) * LOG2E
                        * jnp.max(jnp.abs(attn_q_gain), axis=1) * jnp.max(jnp.abs(attn_k_gain), axis=1)),
    }
    nb = x_prompt.shape[0]
    mod = _modulation(jnp.concatenate([c_prompt, c_sample], axis=0), ada_w, ada_b)
    y_prompt = _run_trunk(x_prompt, mod[:, :nb], w)
    y_sample = _run_trunk(x_sample, mod[:, nb:], w)
    return (y_prompt, y_sample)
```

```python
import functools
import math

import numpy as np
import jax
import jax.numpy as jnp
from jax import lax
from jax.experimental import pallas as pl
from jax.experimental.pallas import tpu as pltpu

F32 = jnp.float32
BF16 = jnp.bfloat16

D_MODEL = 1024
DEPTH = 4
N_HEADS = 16
N_KV_HEADS = 4
KV_GROUP = N_HEADS // N_KV_HEADS
HEAD_DIM = D_MODEL // N_HEADS
Q_DIM = N_HEADS * HEAD_DIM
KV_DIM = N_KV_HEADS * HEAD_DIM
QKV_DIM = Q_DIM + 2 * KV_DIM
GROUP_W = KV_GROUP * HEAD_DIM
ROPE_AXIS_DIM = HEAD_DIM // 2
ROPE_THETA = 10000.0
FNET_GROUPS = 8
FNET_GROUP_DIM = D_MODEL // FNET_GROUPS
D_FF = 2816
GRID_W = 64
N_SUBLAYERS = 3
N_MOD = 3
EPS = 1e-6
LOG2E = 1.4426950408889634

LANES = 128
VMEM_LIMIT = 48 * 1024 * 1024
BIG_VMEM_LIMIT = 56 * 1024 * 1024

FFN_TM = 1024
FFN_GROUP = 512
TOK_TM = 1024
QKV_SUB = 4
ATTN_TQ = 256
ATTN_BLOCKS = 16
ATTN_TK = 2048
ATTN_NOSHIFT_MAX_SCORE = 48.0
SCORE_BOUND_SLACK = 1.02
VT_ROWS = HEAD_DIM + 16
DFT_N2_MIN = 64
DFT_N1_MAX = 64
DFT_KB = 8
DFT_SB = 16


def _params(*sem, vmem_limit=VMEM_LIMIT):
    return pltpu.CompilerParams(dimension_semantics=sem, vmem_limit_bytes=vmem_limit)


def _dot(a, b):
    return jnp.dot(a, b, preferred_element_type=F32)


def _modnorm(x, gain, shift, scale):
    ms = jnp.mean(x * x, axis=-1, keepdims=True)
    return (x * lax.rsqrt(ms + EPS)) * (gain * (1.0 + scale)) + shift


def _mod_kernel(c_ref, w_ref, b_ref, o_ref):
    c = c_ref[...]
    a = (c * jax.nn.sigmoid(c)).astype(BF16)
    o_ref[...] = _dot(a, w_ref[...].astype(BF16)) + b_ref[...]


def _modulation(c, ada_w, ada_b):
    b = c.shape[0]
    n = N_SUBLAYERS * N_MOD
    out = pl.pallas_call(
        _mod_kernel,
        grid=(DEPTH, n),
        in_specs=[
            pl.BlockSpec((b, D_MODEL), lambda l, j: (0, 0)),
            pl.BlockSpec((None, D_MODEL, D_MODEL), lambda l, j: (l, 0, j)),
            pl.BlockSpec((None, 1, D_MODEL), lambda l, j: (l, 0, j)),
        ],
        out_specs=pl.BlockSpec((None, b, D_MODEL), lambda l, j: (l, 0, j)),
        out_shape=jax.ShapeDtypeStruct((DEPTH, b, n * D_MODEL), F32),
        compiler_params=_params("parallel", "parallel"),
        name="adaln_mod",
    )(c, ada_w, ada_b.reshape(DEPTH, 1, n * D_MODEL))
    return out.reshape(DEPTH, b, n, D_MODEL)


def _ffn_update(x, mod, gain, wg_ref, wu_ref, wo_ref, sub):
    h = _modnorm(x, gain, mod[3 * sub:3 * sub + 1], mod[3 * sub + 1:3 * sub + 2]).astype(BF16)
    g = _dot(h, wg_ref[...])
    u = _dot(h, wu_ref[...])
    a = (g * jax.nn.sigmoid(g) * u).astype(BF16)
    return x + (0.5 * mod[3 * sub + 2:3 * sub + 3]) * _dot(a, wo_ref[...])


def _row_groups(tm):
    group = min(FFN_GROUP, tm)
    return [slice(r, r + group) for r in range(0, tm, group)]


def _ffn_kernel(x_ref, mod_ref, gain_ref, wg_ref, wu_ref, wo_ref, o_ref, *, sub):
    for rows in _row_groups(x_ref.shape[0]):
        o_ref[rows, :] = _ffn_update(x_ref[rows, :], mod_ref[...], gain_ref[...], wg_ref, wu_ref, wo_ref, sub)


def _proj_ffn_kernel(x_ref, a_ref, wp_ref, mod_ref, gain_ref, wg_ref, wu_ref, wo_ref, o_ref, *, sub):
    mod = mod_ref[...]
    for rows in _row_groups(x_ref.shape[0]):
        x = x_ref[rows, :] + mod[3 * (sub - 1) + 2:3 * (sub - 1) + 3] * _dot(a_ref[rows, :], wp_ref[...])
        o_ref[rows, :] = _ffn_update(x, mod, gain_ref[...], wg_ref, wu_ref, wo_ref, sub)


def _resident(block_shape, index_map):
    return pl.BlockSpec(block_shape, index_map, pipeline_mode=pl.Buffered(1))


def _ffn(x, mod, gain, w_in, w_out, layer, which, sub, mixed=None, w_proj=None, proj_layer=None):
    b, s, _ = x.shape
    tm = min(FFN_TM, s)
    tok = pl.BlockSpec((None, tm, D_MODEL), lambda i, t: (i, t, 0))
    specs = [
        pl.BlockSpec((None, None, N_SUBLAYERS * N_MOD, D_MODEL), lambda i, t: (layer, i, 0, 0)),
        pl.BlockSpec((1, D_MODEL), lambda i, t: (0, 0)),
        _resident((None, None, D_MODEL, D_FF), lambda i, t: (layer, which, 0, 0)),
        _resident((None, None, D_MODEL, D_FF), lambda i, t: (layer, which, 0, 1)),
        _resident((None, None, D_FF, D_MODEL), lambda i, t: (layer, which, 0, 0)),
    ]
    args = (mod, gain, w_in, w_in, w_out)
    if mixed is None:
        body, specs, args = _ffn_kernel, [tok] + specs, (x,) + args
    else:
        proj = _resident((None, D_MODEL, D_MODEL), lambda i, t: (proj_layer, 0, 0))
        body, specs, args = _proj_ffn_kernel, [tok, tok, proj] + specs, (x, mixed, w_proj) + args
    return pl.pallas_call(
        functools.partial(body, sub=sub),
        grid=(b, s // tm),
        in_specs=specs,
        out_specs=tok,
        out_shape=jax.ShapeDtypeStruct(x.shape, F32),
        compiler_params=_params("parallel", "parallel", vmem_limit=BIG_VMEM_LIMIT),
        name=f"ffn_l{layer}_{which}",
    )(*args)


def _head_rms(t, ones_bd, gain):
    ss = _dot((t * t).astype(BF16), ones_bd)
    return t * lax.rsqrt(ss * (1.0 / HEAD_DIM) + EPS) * gain


def _rope(t, cos, sin_signed):
    n = t.shape[1]
    lane = lax.broadcasted_iota(jnp.int32, t.shape, 1)
    nxt = pltpu.roll(t, n - 1, 1)
    prv = pltpu.roll(t, 1, 1)
    partner = jnp.where(lane % 2 == 0, nxt, prv)
    reps = n // LANES
    cosf = jnp.concatenate([cos] * reps, axis=1)
    sinf = jnp.concatenate([sin_signed] * reps, axis=1)
    return t * cosf + partner * sinf


def _repeat_heads(t):
    lane = lax.broadcasted_iota(jnp.int32, (t.shape[0], LANES), 1)
    low = lane < HEAD_DIM
    cols = []
    for j in range(KV_DIM // LANES):
        c = t[:, j * LANES:(j + 1) * LANES]
        sw = pltpu.roll(c, HEAD_DIM, 1)
        cols += [jnp.where(low, c, sw), jnp.where(low, sw, c)]
    return jnp.concatenate(cols, axis=1)


def _qkv_kernel(x_ref, mod_ref, gain_ref, w_ref, qg_ref, kg_ref, cos_ref, sin_ref, bd_ref,
                q_ref, k_ref, vt_ref):
    mod = mod_ref[...]
    tm = x_ref.shape[0]
    half = tm // QKV_SUB
    for u in range(QKV_SUB):
        rows = slice(u * half, (u + 1) * half)
        h = _modnorm(x_ref[rows, :], gain_ref[...], mod[3:4], mod[4:5]).astype(BF16)
        qkv = _dot(h, w_ref[...])
        q = qkv[:, :Q_DIM]
        k = qkv[:, Q_DIM:Q_DIM + KV_DIM]
        v = qkv[:, Q_DIM + KV_DIM:]
        cos = cos_ref[rows, :]
        sin = sin_ref[rows, :]
        q = _rope(_head_rms(q, bd_ref[...], qg_ref[...]), cos, sin)
        k = _rope(_head_rms(k, bd_ref[:KV_DIM, :KV_DIM], kg_ref[...]), cos, sin)
        q_ref[rows, :] = (q * (LOG2E / math.sqrt(HEAD_DIM))).astype(BF16)
        k_ref[rows, :] = _repeat_heads(k).astype(BF16)
        v_t = v.T
        ones = jnp.ones((VT_ROWS - HEAD_DIM, half), BF16)
        for j in range(N_KV_HEADS):
            vt_ref[j, :HEAD_DIM, rows] = v_t[j * HEAD_DIM:(j + 1) * HEAD_DIM].astype(BF16)
            vt_ref[j, HEAD_DIM:, rows] = ones


def _rope_tables(s):
    rows = s // GRID_W
    r = jnp.repeat(jnp.arange(rows), GRID_W).astype(F32)
    c = jnp.tile(jnp.arange(GRID_W), rows).astype(F32)
    inv = ROPE_THETA ** (-jnp.arange(0, ROPE_AXIS_DIM, 2, dtype=F32) / ROPE_AXIS_DIM)
    ang = jnp.concatenate([r[:, None] * inv, c[:, None] * inv], axis=-1)
    cos = jnp.repeat(jnp.cos(ang), 2, axis=-1)
    sin = jnp.repeat(jnp.sin(ang), 2, axis=-1)
    sign = jnp.tile(jnp.array([-1.0, 1.0], F32), HEAD_DIM // 2)
    reps = LANES // HEAD_DIM
    return jnp.tile(cos, (1, reps)), jnp.tile(sin * sign, (1, reps))


def _qkv(x, mod, gain, w_qkv, q_gain, k_gain, cos, sin, ones_bd, layer, a):
    b, s, _ = x.shape
    tm = min(TOK_TM, s)
    tok = lambda i, t: (i, t, 0)
    const2 = lambda i, t: (0, 0)
    shape = jax.ShapeDtypeStruct((b, s, D_MODEL), BF16)
    tk = min(ATTN_TK, s)
    per = tk // tm
    vt_shape = jax.ShapeDtypeStruct((b, N_KV_HEADS, s // tk, VT_ROWS, tk), BF16)
    vt_spec = pl.BlockSpec((None, N_KV_HEADS, None, VT_ROWS, tm), lambda i, t: (i, 0, t // per, 0, t % per))
    return pl.pallas_call(
        _qkv_kernel,
        grid=(b, s // tm),
        in_specs=[
            pl.BlockSpec((None, tm, D_MODEL), tok),
            pl.BlockSpec((None, None, N_SUBLAYERS * N_MOD, D_MODEL), lambda i, t: (layer, i, 0, 0)),
            pl.BlockSpec((1, D_MODEL), const2),
            pl.BlockSpec((None, D_MODEL, QKV_DIM), lambda i, t: (a, 0, 0)),
            pl.BlockSpec((1, Q_DIM), const2),
            pl.BlockSpec((1, KV_DIM), const2),
            pl.BlockSpec((tm, LANES), lambda i, t: (t, 0)),
            pl.BlockSpec((tm, LANES), lambda i, t: (t, 0)),
            pl.BlockSpec((Q_DIM, Q_DIM), const2),
        ],
        out_specs=[pl.BlockSpec((None, tm, D_MODEL), tok),
                   pl.BlockSpec((None, tm, N_KV_HEADS * LANES), tok), vt_spec],
        out_shape=[shape, jax.ShapeDtypeStruct((b, s, N_KV_HEADS * LANES), BF16), vt_shape],
        compiler_params=_params("parallel", "parallel"),
        name=f"qkv_l{layer}",
    )(x, mod, gain, w_qkv, q_gain, k_gain, cos, sin, ones_bd)


def _attn_kernel(q_ref, k_ref, vt_ref, o_ref, *, tq, nsub, tk, nk, shift):
    head = lax.broadcasted_iota(jnp.int32, (tq, GROUP_W), 1) // HEAD_DIM
    w = KV_GROUP * tq
    qs = []
    for u in range(nsub):
        q = q_ref[u * tq:(u + 1) * tq, :]
        zero = jnp.zeros_like(q)
        qs.append(jnp.concatenate(
            [jnp.where(head == g, q, zero)[:, (g * HEAD_DIM) // LANES * LANES:((g * HEAD_DIM) // LANES + 1) * LANES]
             for g in range(KV_GROUP)], axis=0))

    def scores(c, u):
        kc = k_ref[pl.ds(pl.multiple_of(c * tk, tk), tk), :]
        return lax.dot_general(kc, qs[u], (((1,), (1,)), ((), ())), preferred_element_type=F32)

    def chunk_shifted(c, carry):
        out = []
        for u, (m, acc) in enumerate(carry):
            st = scores(c, u)
            m_new = jnp.maximum(m, jnp.max(st, axis=0, keepdims=True))
            p = jnp.exp2(st - m_new).astype(BF16)
            out.append((m_new, jnp.exp2(m - m_new) * acc + _dot(vt_ref[c], p)))
        return tuple(out)

    def chunk_plain(c, accs):
        return tuple(acc + _dot(vt_ref[c], jnp.exp2(scores(c, u)).astype(BF16)) for u, acc in enumerate(accs))

    acc0 = jnp.zeros((VT_ROWS, w), F32)
    if shift:
        carry = lax.fori_loop(0, nk, chunk_shifted, ((jnp.full((1, w), -1e30, F32), acc0),) * nsub)
        accs = [acc for _, acc in carry]
    else:
        accs = lax.fori_loop(0, nk, chunk_plain, (acc0,) * nsub, unroll=True)
    for u, acc in enumerate(accs):
        out = acc[:HEAD_DIM] / acc[HEAD_DIM:HEAD_DIM + 1]
        o_t = jnp.concatenate([out[:, g * tq:(g + 1) * tq] for g in range(KV_GROUP)], axis=0)
        o_ref[u * tq:(u + 1) * tq, :] = o_t.T.astype(BF16)


def _attention(q, k4, vt, score_bound, layer):
    b, s, _ = q.shape
    nk, tk = vt.shape[2], vt.shape[4]
    tq = min(ATTN_TQ, s)
    nsub = min(ATTN_BLOCKS // nk, s // tq)
    rows = nsub * tq

    def call(shift):
        return pl.pallas_call(
            functools.partial(_attn_kernel, tq=tq, nsub=nsub, tk=tk, nk=nk, shift=shift),
            grid=(b, N_KV_HEADS, s // rows),
            in_specs=[
                pl.BlockSpec((None, rows, GROUP_W), lambda i, j, t: (i, t, j)),
                pl.BlockSpec((None, s, LANES), lambda i, j, t: (i, 0, j)),
                pl.BlockSpec((None, None, nk, VT_ROWS, tk), lambda i, j, t: (i, j, 0, 0, 0)),
            ],
            out_specs=pl.BlockSpec((None, rows, GROUP_W), lambda i, j, t: (i, t, j)),
            out_shape=jax.ShapeDtypeStruct((b, s, D_MODEL), BF16),
            compiler_params=_params("parallel", "parallel", "parallel", vmem_limit=BIG_VMEM_LIMIT),
            name=f"attn_l{layer}_{'shifted' if shift else 'plain'}",
        )

    return lax.cond(score_bound <= ATTN_NOSHIFT_MAX_SCORE,
                    lambda *a: call(False)(*a), lambda *a: call(True)(*a), q, k4, vt)


def _dft_split(s):
    n2 = max(DFT_N2_MIN, s // DFT_N1_MAX)
    return s // n2, n2


def _dft_tables(s):
    n1, n2 = _dft_split(s)
    k1 = np.arange(n1)
    ang1 = 2.0 * np.pi * ((k1[:, None] * k1[None, :]) % n1) / n1
    m1 = np.concatenate([np.cos(ang1), -np.sin(ang1)], axis=0)
    m1 = np.kron(m1, np.eye(DFT_SB))
    src = (np.arange(DFT_KB)[None, :] * n2 + np.arange(n2)[:, None]).reshape(-1)
    perm = np.zeros((DFT_KB * n2, DFT_KB * n2))
    perm[np.arange(DFT_KB * n2), src] = 1.0
    s2 = np.arange(n2)
    num = (s2[None, None, :] * k1[:, None, None] + s2[None, None, :] * s2[None, :, None] * n1) % s
    psi = 2.0 * np.pi * num / s
    tr, ti = np.cos(psi), -np.sin(psi)
    t = np.concatenate([np.concatenate([tr, -ti], axis=2), np.concatenate([ti, tr], axis=2)], axis=1)
    c = np.arange(FNET_GROUP_DIM)
    phi = 2.0 * np.pi * ((c[:, None] * c[None, :]) % FNET_GROUP_DIM) / FNET_GROUP_DIM
    cs = np.concatenate([np.cos(phi), np.sin(phi)], axis=0)
    return jnp.asarray(m1, BF16), jnp.asarray(t, BF16), jnp.asarray(cs, BF16), jnp.asarray(perm, BF16)


def _dft1_kernel(x_ref, mod_ref, gain_ref, m_ref, y_ref, *, n1, sb):
    mod = mod_ref[...]
    x = x_ref[...].reshape(n1 * sb, D_MODEL)
    h = _modnorm(x, gain_ref[...], mod[3:4], mod[4:5]).astype(BF16)
    y_ref[...] = _dot(m_ref[...], h).astype(BF16).reshape(2, n1, sb, D_MODEL)


def _dft1(x, mod, gain, m1, layer):
    b, n1, n2, _ = x.shape
    sb = DFT_SB
    return pl.pallas_call(
        functools.partial(_dft1_kernel, n1=n1, sb=sb),
        grid=(b, n2 // sb),
        in_specs=[
            pl.BlockSpec((None, n1, sb, D_MODEL), lambda i, t: (i, 0, t, 0)),
            pl.BlockSpec((None, None, N_SUBLAYERS * N_MOD, D_MODEL), lambda i, t: (layer, i, 0, 0)),
            pl.BlockSpec((1, D_MODEL), lambda i, t: (0, 0)),
            _resident((2 * n1 * sb, n1 * sb), lambda i, t: (0, 0)),
        ],
        out_specs=pl.BlockSpec((None, 2, n1, sb, D_MODEL), lambda i, t: (i, 0, 0, t, 0)),
        out_shape=jax.ShapeDtypeStruct((b, 2, n1, n2, D_MODEL), BF16),
        compiler_params=_params("parallel", "parallel"),
        name=f"fnet_dft1_l{layer}",
    )(x, mod, gain, m1)


def _dft2_kernel(t_ref, y_ref, x_ref, mod_ref, cs_ref, perm_ref, w_ref, out_ref, u_ref, *, kb, n2, inv_norm):
    for i in range(kb):
        y = jnp.concatenate([y_ref[0, i], y_ref[1, i]], axis=0)
        u = _dot(t_ref[i], y).astype(BF16)
        u_ref[0, i * n2:(i + 1) * n2, :] = u[:n2]
        u_ref[1, i * n2:(i + 1) * n2, :] = u[n2:]
    ur = u_ref[0]
    ui = u_ref[1]
    cs = cs_ref[...]
    parts = []
    for g in range(FNET_GROUPS):
        cols = slice(g * FNET_GROUP_DIM, (g + 1) * FNET_GROUP_DIM)
        parts.append(_dot(jnp.concatenate([ur[:, cols], ui[:, cols]], axis=1), cs))
    f = (jnp.concatenate(parts, axis=1) * inv_norm).astype(BF16)
    f = _dot(perm_ref[...], f).astype(BF16)
    r = mod_ref[...][5:6] * _dot(f, w_ref[...])
    out_ref[...] = x_ref[...] + r.reshape(n2, kb, D_MODEL)


def _dft2_residual(y, x, mod, t, cs, perm, w_o, layer, a):
    b, _, n1, n2, _ = y.shape
    kb = DFT_KB
    inv_norm = 1.0 / math.sqrt(n1 * n2 * FNET_GROUP_DIM)
    return pl.pallas_call(
        functools.partial(_dft2_kernel, kb=kb, n2=n2, inv_norm=inv_norm),
        grid=(b, n1 // kb),
        in_specs=[
            pl.BlockSpec((kb, 2 * n2, 2 * n2), lambda i, t: (t, 0, 0)),
            pl.BlockSpec((None, 2, kb, n2, D_MODEL), lambda i, t: (i, 0, t, 0, 0)),
            pl.BlockSpec((None, n2, kb, D_MODEL), lambda i, t: (i, 0, t, 0)),
            pl.BlockSpec((None, None, N_SUBLAYERS * N_MOD, D_MODEL), lambda i, t: (layer, i, 0, 0)),
            pl.BlockSpec((2 * FNET_GROUP_DIM, FNET_GROUP_DIM), lambda i, t: (0, 0)),
            pl.BlockSpec((kb * n2, kb * n2), lambda i, t: (0, 0)),
            pl.BlockSpec((None, D_MODEL, D_MODEL), lambda i, t: (a, 0, 0)),
        ],
        out_specs=pl.BlockSpec((None, n2, kb, D_MODEL), lambda i, t: (i, 0, t, 0)),
        out_shape=jax.ShapeDtypeStruct(x.shape, F32),
        scratch_shapes=[pltpu.VMEM((2, kb * n2, D_MODEL), BF16)],
        compiler_params=_params("parallel", "parallel"),
        name=f"fnet_dft2_l{layer}",
    )(t, y, x, mod, cs, perm, w_o)


def _fourier_mix(x, mod, gain, w_o, tables, layer, a):
    b, s, _ = x.shape
    m1, t, cs, perm = tables
    n1, n2 = _dft_split(s)
    y = _dft1(x.reshape(b, n1, n2, D_MODEL), mod, gain, m1, layer)
    out = _dft2_residual(y, x.reshape(b, n2, n1, D_MODEL), mod, t, cs, perm, w_o, layer, a)
    return out.reshape(b, s, D_MODEL)


def _run_trunk(x, mod, w):
    s = x.shape[1]
    cos, sin = _rope_tables(s)
    tables = _dft_tables(s)
    for l in range(DEPTH):
        gains = w["norm_gain"][l]
        x = _ffn(x, mod, gains[0:1], w["ffn_w_in"], w["ffn_w_out"], l, 0, 0)
        a = l // 2
        if l % 2 == 0:
            q, k4, vt = _qkv(x, mod, gains[1:2], w["attn_w_qkv"], w["q_gain"][a], w["k_gain"][a],
                             cos, sin, w["ones_bd"], l, a)
            o = _attention(q, k4, vt, w["score_bound"][a], l)
            x = _ffn(x, mod, gains[2:3], w["ffn_w_in"], w["ffn_w_out"], l, 1, 2,
                     mixed=o, w_proj=w["attn_w_o"], proj_layer=a)
        else:
            x = _fourier_mix(x, mod, gains[1:2], w["fnet_w_o"], tables, l, a)
            x = _ffn(x, mod, gains[2:3], w["ffn_w_in"], w["ffn_w_out"], l, 1, 2)
    return x


def kernel(x_prompt, x_sample, c_prompt, c_sample, norm_gain, ada_w, ada_b, ffn_w_in, ffn_w_out,
           attn_w_qkv, attn_q_gain, attn_k_gain, attn_w_o, fnet_w_o):
    head = np.arange(Q_DIM) // HEAD_DIM
    w = {
        "norm_gain": norm_gain,
        "ffn_w_in": ffn_w_in.astype(BF16),
        "ffn_w_out": ffn_w_out.astype(BF16),
        "attn_w_qkv": attn_w_qkv.astype(BF16),
        "attn_w_o": attn_w_o.astype(BF16),
        "fnet_w_o": fnet_w_o.astype(BF16),
        "q_gain": jnp.tile(attn_q_gain, (1, N_HEADS))[:, None, :],
        "k_gain": jnp.tile(attn_k_gain, (1, N_KV_HEADS))[:, None, :],
        "ones_bd": jnp.asarray(head[:, None] == head[None, :], BF16),
        "score_bound": (SCORE_BOUND_SLACK * math.sqrt(HEAD_DIM) * LOG2E
                        * jnp.max(jnp.abs(attn_q_gain), axis=1) * jnp.max(jnp.abs(attn_k_gain), axis=1)),
    }
    nb = x_prompt.shape[0]
    mod = _modulation(jnp.concatenate([c_prompt, c_sample], axis=0), ada_w, ada_b)
    y_prompt = _run_trunk(x_prompt, mod[:, :nb], w)
    y_sample = _run_trunk(x_sample, mod[:, nb:], w)
    return (y_prompt, y_sample)
```
